```python
import math
import jax, jax.numpy as jnp
from jax import lax
import numpy as np

D_MODEL = 1024
BATCH = 1
SEQ = 16384
DEPTH = 1
DEC_BATCH = 16
DEC_SEQ = 16
PAST_LEN = 2048

CHUNK = 64
MIX = D_MODEL
H_A = 4
DK_A = MIX // (2 * H_A)
DV_A = DK_A
CONV_W = 4
H_B = 4
DH_B = MIX // (4 * H_B)
DV_B = 2 * DH_B
Q_BLOCK = 128
DIFF_EPS = 1e-5
N_KEYS = 128
N_EXPERTS = N_KEYS * N_KEYS
PEER_HEADS = 8
PEER_TOPK = 16
PEER_DK = 128
PEER_DK_HALF = PEER_DK // 2
PEER_BLOCK = 128
EPS = 1e-6

W_A = H_A * DK_A
CONV_CH = 3 * W_A
O_Z = CONV_CH
O_BETA = O_Z + H_A * DV_A
O_ALPHA = O_BETA + H_A
O_QB = O_ALPHA + H_A
W_QK_B = H_B * 2 * DH_B
O_KB = O_QB + W_QK_B
O_VB = O_KB + W_QK_B
IN_COLS = O_VB + H_B * DV_B
OUT_IN = H_A * DV_A + H_B * DV_B

kernel_name = "hybrid_gdn_diffattn_peer_stream_step"


def rmsnorm(x, w, eps=EPS):
    xf = x.astype(jnp.float32)
    y = xf * lax.rsqrt(jnp.mean(xf * xf, axis=-1, keepdims=True) + eps) * w.astype(jnp.float32)
    return y.astype(x.dtype)


def l2norm(x, eps=EPS):
    xf = x.astype(jnp.float32)
    return xf * lax.rsqrt(jnp.sum(xf * xf, axis=-1, keepdims=True) + eps)


def causal_short_conv(u, prev, w):
    T = u.shape[1]
    up = jnp.concatenate([prev.astype(u.dtype), u], axis=1)
    y = sum(up[:, j:j + T] * w[j] for j in range(CONV_W))
    return jax.nn.silu(y), up[:, T:]


def gated_delta_rule(q, k, v, g, beta, s0):
    B, T, H, DK = q.shape
    DV = v.shape[-1]
    C = min(CHUNK, T)
    n = T // C

    def chunks(a):
        a = a.reshape((B, n, C, H) + a.shape[3:])
        return jnp.moveaxis(jnp.swapaxes(a, 2, 3), 1, 0)

    q = chunks(q * DK ** -0.5)
    k = chunks(k)
    v = chunks(v)
    g = chunks(g)
    beta = chunks(beta)
    gc = jnp.cumsum(g, axis=-1)
    incl = jnp.tril(jnp.ones((C, C), bool))
    strict = jnp.tril(jnp.ones((C, C), bool), -1)
    diff = gc[..., :, None] - gc[..., None, :]
    decay = jnp.where(incl, jnp.exp(jnp.where(incl, diff, 0.0)), 0.0)
    kb = k * beta[..., None]
    lower = jnp.where(strict, jnp.einsum('nbhik,nbhjk->nbhij', kb, k) * decay, 0.0)
    eye = jnp.eye(C, dtype=jnp.float32)
    tinv = lax.linalg.triangular_solve(eye + lower, jnp.broadcast_to(eye, lower.shape),
                                       left_side=True, lower=True)
    u = tinv @ (v * beta[..., None])
    w = tinv @ (kb * jnp.exp(gc)[..., None])

    def step(S, xs):
        qc, kc, uc, wc, gcc, dc = xs
        attn = jnp.einsum('bhik,bhjk->bhij', qc, kc) * dc
        v_new = uc - jnp.einsum('bhck,bhkv->bhcv', wc, S)
        o = (jnp.einsum('bhck,bhkv->bhcv', qc * jnp.exp(gcc)[..., None], S)
             + jnp.einsum('bhij,bhjv->bhiv', attn, v_new))
        g_last = gcc[..., -1]
        S = (S * jnp.exp(g_last)[..., None, None]
             + jnp.einsum('bhck,bhcv->bhkv', kc * jnp.exp(g_last[..., None] - gcc)[..., None], v_new))
        return S, o

    S, o = lax.scan(step, s0, (q, k, u, w, gc, decay))
    o = jnp.swapaxes(jnp.moveaxis(o, 0, 1), 2, 3).reshape(B, T, H, DV)
    return o, S


def diff_attention_prompt(q, k, v, lam):
    B, T = q.shape[:2]
    nb = T // Q_BLOCK
    qb = jnp.moveaxis(q.reshape(B, nb, Q_BLOCK, H_B, 2, DH_B), 1, 0)
    key_chunk = jnp.arange(T) // CHUNK

    def one(args):
        qi, i = args
        s = jnp.einsum('bqhmd,bkhmd->bhmqk', qi, k).astype(jnp.float32) * DH_B ** -0.5
        q_chunk = (i * Q_BLOCK + jnp.arange(Q_BLOCK)) // CHUNK
        mask = key_chunk[None, :] <= q_chunk[:, None]
        pr = jax.nn.softmax(jnp.where(mask, s, -jnp.inf), axis=-1)
        a = pr[:, :, 0] - lam * pr[:, :, 1]
        return jnp.einsum('bhqk,bkhv->bqhv', a, v)

    o = lax.map(one, (qb, jnp.arange(nb)))
    return jnp.moveaxis(o, 0, 1).reshape(B, T, H_B, DV_B)


def diff_attention_sample(q, k, v, k_cache, v_cache, lam):
    B, P = k_cache.shape[:2]
    k_all = jnp.concatenate([k_cache.reshape(B, P, H_B, 2, DH_B).astype(k.dtype), k], axis=1)
    v_all = jnp.concatenate([v_cache.astype(v.dtype), v], axis=1)
    s = jnp.einsum('bqhmd,bkhmd->bhmqk', q, k_all).astype(jnp.float32) * DH_B ** -0.5
    pr = jax.nn.softmax(s, axis=-1)
    a = pr[:, :, 0] - lam * pr[:, :, 1]
    return jnp.einsum('bhqk,bkhv->bqhv', a, v_all)


def peer_ffn(h, wq, sub_k1, sub_k2, u, v):
    B, T, D = h.shape
    n_tok = B * T
    n_pad = (-n_tok) % PEER_BLOCK
    xs = jnp.pad(h.reshape(n_tok, D), ((0, n_pad), (0, 0))).reshape(-1, PEER_BLOCK, D)

    def block(xb):
        q = (xb @ wq).reshape(PEER_BLOCK, PEER_HEADS, 2, PEER_DK_HALF)
        s1 = jnp.einsum('thd,nd->thn', q[:, :, 0], sub_k1).astype(jnp.float32)
        s2 = jnp.einsum('thd,nd->thn', q[:, :, 1], sub_k2).astype(jnp.float32)
        v1, i1 = lax.top_k(s1, PEER_TOPK)
        v2, i2 = lax.top_k(s2, PEER_TOPK)
        cand_s = (v1[..., :, None] + v2[..., None, :]).reshape(PEER_BLOCK, PEER_HEADS, PEER_TOPK * PEER_TOPK)
        cand_i = (i1[..., :, None] * N_KEYS + i2[..., None, :]).reshape(PEER_BLOCK, PEER_HEADS, PEER_TOPK * PEER_TOPK)
        top_s, pos = lax.top_k(cand_s, PEER_TOPK)
        idx = jnp.take_along_axis(cand_i, pos, axis=-1)
        gate = jax.nn.softmax(top_s, axis=-1)
        act = jax.nn.gelu(jnp.einsum('td,thkd->thk', xb, u[idx]).astype(jnp.float32), approximate=False)
        return jnp.einsum('thk,thkd->td', (gate * act).astype(xb.dtype), v[idx])

    y = lax.map(block, xs).reshape(-1, D)[:n_tok]
    return y.reshape(B, T, D)


def encoder_layer(x, conv_prev, ssm_prev, k_cache, v_cache, lam_init,
                  norm1_w, w_in, conv_w, a_log, dt_bias, gdn_norm_w,
                  lam_q1, lam_k1, lam_q2, lam_k2, subln_w, w_out,
                  norm2_w, peer_wq, peer_k1, peer_k2, peer_u, peer_v):
    B, T, _ = x.shape
    f32 = jnp.float32
    h = rmsnorm(x, norm1_w)
    p = h @ w_in
    qkv, conv_new = causal_short_conv(p[..., :CONV_CH], conv_prev, conv_w)
    qkv = qkv.reshape(B, T, 3, H_A, DK_A)
    z = p[..., O_Z:O_BETA].reshape(B, T, H_A, DV_A)
    beta = jax.nn.sigmoid(p[..., O_BETA:O_ALPHA].astype(f32))
    g = -jnp.exp(a_log.astype(f32)) * jax.nn.softplus(p[..., O_ALPHA:O_QB].astype(f32) + dt_bias.astype(f32))
    o_a, ssm_new = gated_delta_rule(l2norm(qkv[:, :, 0]), l2norm(qkv[:, :, 1]),
                                    qkv[:, :, 2].astype(f32), g, beta, ssm_prev.astype(f32))
    o_a = rmsnorm(o_a, gdn_norm_w) * jax.nn.silu(z.astype(f32))
    q_b = p[..., O_QB:O_KB].reshape(B, T, H_B, 2, DH_B)
    k_b = p[..., O_KB:O_VB].reshape(B, T, H_B, 2, DH_B)
    v_b = p[..., O_VB:IN_COLS].reshape(B, T, H_B, DV_B)
    lam = (jnp.exp(jnp.sum(lam_q1.astype(f32) * lam_k1.astype(f32)))
           - jnp.exp(jnp.sum(lam_q2.astype(f32) * lam_k2.astype(f32))) + lam_init)
    if k_cache is None:
        o_b = diff_attention_prompt(q_b, k_b, v_b, lam)
    else:
        o_b = diff_attention_sample(q_b, k_b, v_b, k_cache, v_cache, lam)
    o_b = rmsnorm(o_b, subln_w, DIFF_EPS) * (1.0 - lam_init)
    o = jnp.concatenate([o_a.reshape(B, T, -1), o_b.reshape(B, T, -1)], axis=-1).astype(x.dtype)
    x = x + o @ w_out
    x = x + peer_ffn(rmsnorm(x, norm2_w), peer_wq, peer_k1, peer_k2, peer_u, peer_v)
    return x, k_b.reshape(B, T, H_B, 2 * DH_B), v_b, conv_new, ssm_new.astype(x.dtype)


def setup_inputs(seed: int = 0) -> dict:
    key = jax.random.key(seed)
    ks = jax.random.split(key, 28)
    f32 = jnp.float32
    L = DEPTH
    D = D_MODEL

    def nrm(k, shape, scale):
        return jax.random.normal(k, shape, f32) * scale

    dt = jnp.exp(jax.random.uniform(ks[10], (L, H_A), f32, float(np.log(1e-3)), float(np.log(1e-1))))
    return {
        "x_prompt": nrm(ks[0], (BATCH, SEQ, D), 1.0),
        "x_sample": nrm(ks[1], (DEC_BATCH, DEC_SEQ, D), 1.0),
        "cache_k": nrm(ks[2], (L, DEC_BATCH, PAST_LEN, H_B, 2 * DH_B), 1.0),
        "cache_v": nrm(ks[3], (L, DEC_BATCH, PAST_LEN, H_B, DV_B), 1.0),
        "state_conv": nrm(ks[4], (L, DEC_BATCH, CONV_W - 1, CONV_CH), 1.0),
        "state_gdn": nrm(ks[5], (L, DEC_BATCH, H_A, DK_A, DV_A), 0.1),
        "norm1_w": 1.0 + nrm(ks[6], (L, D), 0.02),
        "w_in": nrm(ks[7], (L, D, IN_COLS), D ** -0.5),
        "conv_w": nrm(ks[8], (L, CONV_W, CONV_CH), CONV_W ** -0.5),
        "a_log": jnp.log(jax.random.uniform(ks[9], (L, H_A), f32, 1.0, 16.0)),
        "dt_bias": dt + jnp.log(-jnp.expm1(-dt)),
        "gdn_norm_w": 1.0 + nrm(ks[11], (L, DV_A), 0.02),
        "lam_q1": nrm(ks[12], (L, DH_B), 0.1),
        "lam_k1": nrm(ks[13], (L, DH_B), 0.1),
        "lam_q2": nrm(ks[14], (L, DH_B), 0.1),
        "lam_k2": nrm(ks[15], (L, DH_B), 0.1),
        "subln_w": 1.0 + nrm(ks[16], (L, DV_B), 0.02),
        "w_out": nrm(ks[17], (L, OUT_IN, D), OUT_IN ** -0.5),
        "norm2_w": 1.0 + nrm(ks[18], (L, D), 0.02),
        "peer_wq": nrm(ks[19], (L, D, PEER_HEADS * PEER_DK), D ** -0.5),
        "peer_k1": nrm(ks[20], (L, N_KEYS, PEER_DK_HALF), PEER_DK_HALF ** -0.5),
        "peer_k2": nrm(ks[21], (L, N_KEYS, PEER_DK_HALF), PEER_DK_HALF ** -0.5),
        "peer_u": nrm(ks[22], (L, N_EXPERTS, D), D ** -0.5),
        "peer_v": nrm(ks[23], (L, N_EXPERTS, D), PEER_HEADS ** -0.5),
        "norm_f_w": 1.0 + nrm(ks[24], (D,), 0.02),
    }


def reference(x_prompt, x_sample, cache_k, cache_v, state_conv, state_gdn,
              norm1_w, w_in, conv_w, a_log, dt_bias, gdn_norm_w,
              lam_q1, lam_k1, lam_q2, lam_k2, subln_w, w_out,
              norm2_w, peer_wq, peer_k1, peer_k2, peer_u, peer_v, norm_f_w):
    yp, ys = x_prompt, x_sample
    kp, vp, cp, sp, kd, vd, cd, sd = [], [], [], [], [], [], [], []
    for l in range(DEPTH):
        lam_init = 0.8 - 0.6 * math.exp(-0.3 * l)
        lw = (norm1_w[l], w_in[l], conv_w[l], a_log[l], dt_bias[l], gdn_norm_w[l],
              lam_q1[l], lam_k1[l], lam_q2[l], lam_k2[l], subln_w[l], w_out[l],
              norm2_w[l], peer_wq[l], peer_k1[l], peer_k2[l], peer_u[l], peer_v[l])
        conv0 = jnp.zeros((yp.shape[0], CONV_W - 1, CONV_CH), yp.dtype)
        ssm0 = jnp.zeros((yp.shape[0], H_A, DK_A, DV_A), jnp.float32)
        yp, k1, v1, c1, s1 = encoder_layer(yp, conv0, ssm0, None, None, lam_init, *lw)
        ys, k2, v2, c2, s2 = encoder_layer(ys, state_conv[l], state_gdn[l], cache_k[l], cache_v[l], lam_init, *lw)
        kp.append(k1); vp.append(v1); cp.append(c1); sp.append(s1)
        kd.append(k2); vd.append(v2); cd.append(c2); sd.append(s2)
    y_prompt = rmsnorm(yp, norm_f_w)
    y_sample = rmsnorm(ys, norm_f_w)
    return (y_prompt, y_sample,
            jnp.stack(kp), jnp.stack(vp), jnp.stack(cp), jnp.stack(sp),
            jnp.stack(kd), jnp.stack(vd), jnp.stack(cd), jnp.stack(sd))
```

```python
import functools
import math

import jax
import jax.numpy as jnp
from jax import lax
from jax.experimental import pallas as pl
from jax.experimental.pallas import tpu as pltpu

F32 = jnp.float32
BF16 = jnp.bfloat16
HIGHEST = lax.Precision.HIGHEST

D_MODEL = 1024
CHUNK = 64
H_A = 4
DK_A = 128
CONV_W = 4
W_A = H_A * DK_A
CONV_CH = 3 * W_A
H_B = 4
DH_B = 64
DV_B = 128
W_B = H_B * DV_B
DIFF_EPS = 1e-5
N_KEYS = 128
N_EXPERTS = N_KEYS * N_KEYS
PEER_HEADS = 8
PEER_TOPK = 16
PEER_DK_HALF = 64
EPS = 1e-6
LANES = 128
SUBLANES = 8
VMEM_LIMIT = 56 * 1024 * 1024

NT_DIMS = (((1,), (1,)), ((), ()))
TN_DIMS = (((0,), (0,)), ((), ()))


def _sigmoid(x):
    return 1.0 / (1.0 + jnp.exp(-x))


def _softplus(x):
    return jnp.maximum(x, 0.0) + jnp.log(1.0 + jnp.exp(-jnp.abs(x)))


def _dot_hi(a, b):
    return jnp.dot(a, b, precision=HIGHEST, preferred_element_type=F32)


def _inproj_kernel(x_ref, n1_ref, w_ref, wba_ref, wbat_ref,
                   conv_ref, z_ref, qb_ref, kb_ref, vb_ref, kbh_ref, vbh_ref, ba_ref, bat_ref):
    x = x_ref[...]
    ms = jnp.mean(x * x, axis=-1, keepdims=True)
    h = (x * lax.rsqrt(ms + EPS) * n1_ref[...]).astype(BF16)

    def mm(c0, c1):
        return jnp.dot(h, w_ref[:, c0:c1], preferred_element_type=F32)

    conv_ref[...] = mm(0, CONV_CH)
    z_ref[...] = mm(CONV_CH, CONV_CH + W_A)
    o = CONV_CH + W_A
    qb_ref[...] = (mm(o, o + W_B) * (DH_B ** -0.5)).astype(BF16)
    kb = mm(o + W_B, o + 2 * W_B)
    kb_ref[...] = kb
    kbh_ref[...] = kb.astype(BF16)
    vb = mm(o + 2 * W_B, o + 3 * W_B)
    vb_ref[...] = vb
    vbh_ref[...] = vb.astype(BF16)
    ba_ref[...] = jnp.dot(h, wba_ref[...], preferred_element_type=F32)
    bat_ref[...] = lax.dot_general(wbat_ref[...], h, NT_DIMS, preferred_element_type=F32)


def _inproj(x2d, n1, w_main, w_ba, w_bat, tm):
    T = x2d.shape[0]
    row = lambda c: pl.BlockSpec((tm, c), lambda i: (i, 0))
    full = lambda a: pl.BlockSpec(a.shape, lambda i: (0, 0))
    out_shape = (
        jax.ShapeDtypeStruct((T, CONV_CH), F32),
        jax.ShapeDtypeStruct((T, W_A), F32),
        jax.ShapeDtypeStruct((T, W_B), BF16),
        jax.ShapeDtypeStruct((T, W_B), F32),
        jax.ShapeDtypeStruct((T, W_B), F32),
        jax.ShapeDtypeStruct((T, W_B), BF16),
        jax.ShapeDtypeStruct((T, W_B), BF16),
        jax.ShapeDtypeStruct((T, LANES), F32),
        jax.ShapeDtypeStruct((SUBLANES, T), F32),
    )
    out_specs = (row(CONV_CH), row(W_A), row(W_B), row(W_B), row(W_B), row(W_B), row(W_B), row(LANES),
                 pl.BlockSpec((SUBLANES, tm), lambda i: (0, i)))
    return pl.pallas_call(
        _inproj_kernel,
        out_shape=out_shape,
        grid=(T // tm,),
        in_specs=[row(D_MODEL), full(n1), full(w_main), full(w_ba), full(w_bat)],
        out_specs=out_specs,
        compiler_params=pltpu.CompilerParams(dimension_semantics=("arbitrary",), vmem_limit_bytes=VMEM_LIMIT),
        name="inproj",
    )(x2d, n1, w_main, w_ba, w_bat)


def _gdn_kernel(conv_ref, z_ref, ba_ref, bat_ref, prev_ref, s0_ref, cw_ref, arow_ref, drow_ref,
                acol_ref, dcol_ref, nw_ref, o_ref, sout_ref, up_ref, s_ref, *, C):
    t = pl.program_id(1)

    @pl.when(t == 0)
    def _():
        up_ref[0:SUBLANES, :] = prev_ref[...]
        s_ref[...] = s0_ref[...]

    up_ref[SUBLANES:SUBLANES + C, :] = conv_ref[...]
    y = up_ref[pl.ds(SUBLANES - CONV_W + 1, C), :] * cw_ref[0:1, :]
    for j in range(1, CONV_W):
        y = y + up_ref[pl.ds(SUBLANES - CONV_W + 1 + j, C), :] * cw_ref[j:j + 1, :]
    qkv = y * _sigmoid(y)
    tail = up_ref[C:C + SUBLANES, :]
    up_ref[0:SUBLANES, :] = tail

    ri = lax.broadcasted_iota(jnp.int32, (C, C), 0)
    ci = lax.broadcasted_iota(jnp.int32, (C, C), 1)
    incl = ri >= ci
    strict = ri > ci
    tril = incl.astype(F32)
    triu = (ri <= ci).astype(F32)
    eye = (ri == ci).astype(F32)

    ba = ba_ref[...]
    beta_all = _sigmoid(ba)
    g_all = -jnp.exp(arow_ref[...]) * _softplus(ba + drow_ref[...])
    gc_all = _dot_hi(tril, g_all)
    bat = bat_ref[...]
    g_t = -jnp.exp(acol_ref[...]) * _softplus(bat + dcol_ref[...])
    gc_t = _dot_hi(g_t, triu)

    nw = nw_ref[...]
    for h in range(H_A):
        gcc = gc_all[:, H_A + h:H_A + h + 1]
        gcr = gc_t[H_A + h:H_A + h + 1, :]
        beta = beta_all[:, h:h + 1]
        decay = jnp.where(incl, jnp.exp(jnp.where(incl, gcc - gcr, 0.0)), 0.0)
        qh = qkv[:, h * DK_A:(h + 1) * DK_A]
        kh = qkv[:, W_A + h * DK_A:W_A + (h + 1) * DK_A]
        vh = qkv[:, 2 * W_A + h * DK_A:2 * W_A + (h + 1) * DK_A]
        qh = qh * lax.rsqrt(jnp.sum(qh * qh, axis=-1, keepdims=True) + EPS) * (DK_A ** -0.5)
        kh = kh * lax.rsqrt(jnp.sum(kh * kh, axis=-1, keepdims=True) + EPS)
        kb = kh * beta
        a = lax.dot_general(kb, kh, NT_DIMS, precision=HIGHEST, preferred_element_type=F32)
        n = jnp.where(strict, -(a * decay), 0.0)
        tinv = eye + n
        p = n
        for _ in range(int(math.log2(C)) - 1):
            p = _dot_hi(p, p)
            tinv = tinv + _dot_hi(tinv, p)
        egc = jnp.exp(gcc)
        u = _dot_hi(tinv, vh * beta)
        w = _dot_hi(tinv, kb * egc)
        attn = lax.dot_general(qh, kh, NT_DIMS, precision=HIGHEST, preferred_element_type=F32) * decay
        s = s_ref[h]
        v_new = u - _dot_hi(w, s)
        o = _dot_hi(qh * egc, s) + _dot_hi(attn, v_new)
        g_last = gcc[C - 1:C, :]
        kd = kh * jnp.exp(g_last - gcc)
        s_ref[h] = s * jnp.exp(g_last) + lax.dot_general(kd, v_new, TN_DIMS, precision=HIGHEST,
                                                         preferred_element_type=F32)
        on = o * lax.rsqrt(jnp.mean(o * o, axis=-1, keepdims=True) + EPS) * nw
        zz = z_ref[:, h * DK_A:(h + 1) * DK_A]
        o_ref[:, h * DK_A:(h + 1) * DK_A] = (on * (zz * _sigmoid(zz))).astype(o_ref.dtype)

    @pl.when(t == pl.num_programs(1) - 1)
    def _():
        sout_ref[...] = s_ref[...]


def _gdn(conv_in, z, ba, bat, prev8, s0, conv_w, arow, drow, acol, dcol, nw, C):
    B, T, _ = conv_in.shape
    tok = lambda c: pl.BlockSpec((None, C, c), lambda b, t: (b, t, 0))
    full2 = lambda a: pl.BlockSpec(a.shape, lambda b, t: (0, 0))
    return pl.pallas_call(
        functools.partial(_gdn_kernel, C=C),
        out_shape=(jax.ShapeDtypeStruct((B, T, W_A), BF16),
                   jax.ShapeDtypeStruct((B, H_A, DK_A, DK_A), F32)),
        grid=(B, T // C),
        in_specs=[tok(CONV_CH), tok(W_A), tok(LANES),
                  pl.BlockSpec((None, None, SUBLANES, C), lambda b, t: (b, t, 0, 0)),
                  pl.BlockSpec((None, SUBLANES, CONV_CH), lambda b, t: (b, 0, 0)),
                  pl.BlockSpec((None, H_A, DK_A, DK_A), lambda b, t: (b, 0, 0, 0)),
                  full2(conv_w), full2(arow), full2(drow), full2(acol), full2(dcol), full2(nw)],
        out_specs=(tok(W_A), pl.BlockSpec((None, H_A, DK_A, DK_A), lambda b, t: (b, 0, 0, 0))),
        scratch_shapes=[pltpu.VMEM((SUBLANES + C, CONV_CH), F32), pltpu.VMEM((H_A, DK_A, DK_A), F32)],
        compiler_params=pltpu.CompilerParams(dimension_semantics=("arbitrary", "arbitrary"),
                                             vmem_limit_bytes=VMEM_LIMIT),
        name="gdn",
    )(conv_in, z, ba, bat, prev8, s0, conv_w, arow, drow, acol, dcol, nw)


def _split_maps(q):
    lane = lax.broadcasted_iota(jnp.int32, q.shape, 1)
    zero = jnp.zeros_like(q)
    return jnp.concatenate([jnp.where(lane < DH_B, q, zero), jnp.where(lane >= DH_B, q, zero)], axis=0)


def _subln(acc, l, lam, sw, tq, lam_init):
    o = acc[:tq] / l[:tq] - lam * (acc[tq:] / l[tq:])
    return o * lax.rsqrt(jnp.mean(o * o, axis=-1, keepdims=True) + DIFF_EPS) * sw * (1.0 - lam_init)


def _attn_prompt_kernel(lam_ref, q_ref, k_ref, v_ref, sw_ref, o_ref, m_ref, l_ref, acc_ref, *, tq, lam_init):
    qi = pl.program_id(1)
    qs = _split_maps(q_ref[...])
    m_ref[...] = jnp.full(m_ref.shape, -jnp.inf, F32)
    l_ref[...] = jnp.zeros(l_ref.shape, F32)
    acc_ref[...] = jnp.zeros(acc_ref.shape, F32)

    def block(j, masked):
        start = pl.multiple_of(j * tq, tq)
        kblk = k_ref[pl.ds(start, tq), :]
        vblk = v_ref[pl.ds(start, tq), :]
        s = lax.dot_general(qs, kblk, NT_DIMS, preferred_element_type=F32)
        if masked:
            r = lax.broadcasted_iota(jnp.int32, s.shape, 0)
            c = lax.broadcasted_iota(jnp.int32, s.shape, 1)
            qpos = jnp.where(r >= tq, r - tq, r)
            s = jnp.where((c // CHUNK) <= (qpos // CHUNK), s, -jnp.inf)
        m_old = m_ref[...]
        m_new = jnp.maximum(m_old, jnp.max(s, axis=-1, keepdims=True))
        alpha = jnp.exp(m_old - m_new)
        p = jnp.exp(s - m_new)
        l_ref[...] = alpha * l_ref[...] + jnp.sum(p, axis=-1, keepdims=True)
        acc_ref[...] = alpha * acc_ref[...] + jnp.dot(p.astype(BF16), vblk, preferred_element_type=F32)
        m_ref[...] = m_new

    def body(j, carry):
        block(j, False)
        return carry

    lax.fori_loop(0, qi, body, 0)
    block(qi, True)
    o_ref[...] = _subln(acc_ref[...], l_ref[...], lam_ref[0], sw_ref[...], tq, lam_init).astype(o_ref.dtype)


def _attn_prompt(lam, q, k, v, sw, tq, lam_init):
    T = q.shape[0]
    return pl.pallas_call(
        functools.partial(_attn_prompt_kernel, tq=tq, lam_init=lam_init),
        out_shape=jax.ShapeDtypeStruct((T, W_B), BF16),
        grid=(H_B, T // tq),
        in_specs=[pl.BlockSpec(memory_space=pltpu.SMEM),
                  pl.BlockSpec((tq, DV_B), lambda h, i: (i, h)),
                  pl.BlockSpec((T, DV_B), lambda h, i: (0, h)),
                  pl.BlockSpec((T, DV_B), lambda h, i: (0, h)),
                  pl.BlockSpec((1, DV_B), lambda h, i: (0, 0))],
        out_specs=pl.BlockSpec((tq, DV_B), lambda h, i: (i, h)),
        scratch_shapes=[pltpu.VMEM((2 * tq, 1), F32), pltpu.VMEM((2 * tq, 1), F32),
                        pltpu.VMEM((2 * tq, DV_B), F32)],
        compiler_params=pltpu.CompilerParams(dimension_semantics=("arbitrary", "arbitrary"),
                                             vmem_limit_bytes=VMEM_LIMIT),
        name="attn_prompt",
    )(lam, q, k, v, sw)


def _attn_sample_kernel(lam_ref, q_ref, kn_ref, vn_ref, ck_ref, cv_ref, sw_ref, o_ref, *, tq, lam_init):
    qs = _split_maps(q_ref[...])
    kc = ck_ref[...].astype(BF16)
    vc = cv_ref[...].astype(BF16)
    sc = lax.dot_general(qs, kc, NT_DIMS, preferred_element_type=F32)
    sn = lax.dot_general(qs, kn_ref[...], NT_DIMS, preferred_element_type=F32)
    m = jnp.maximum(jnp.max(sc, axis=-1, keepdims=True), jnp.max(sn, axis=-1, keepdims=True))
    pc = jnp.exp(sc - m)
    pn = jnp.exp(sn - m)
    l = jnp.sum(pc, axis=-1, keepdims=True) + jnp.sum(pn, axis=-1, keepdims=True)
    acc = (jnp.dot(pc.astype(BF16), vc, preferred_element_type=F32)
           + jnp.dot(pn.astype(BF16), vn_ref[...], preferred_element_type=F32))
    o_ref[...] = _subln(acc, l, lam_ref[0], sw_ref[...], tq, lam_init).astype(o_ref.dtype)


def _attn_sample(lam, q, kn, vn, ck, cv, sw, lam_init):
    B, tq, _ = q.shape
    P = ck.shape[1]
    new = pl.BlockSpec((None, tq, DV_B), lambda b, h: (b, 0, h))
    cache = pl.BlockSpec((None, P, DV_B), lambda b, h: (b, 0, h))
    return pl.pallas_call(
        functools.partial(_attn_sample_kernel, tq=tq, lam_init=lam_init),
        out_shape=jax.ShapeDtypeStruct((B, tq, W_B), BF16),
        grid=(B, H_B),
        in_specs=[pl.BlockSpec(memory_space=pltpu.SMEM), new, new, new, cache, cache,
                  pl.BlockSpec((1, DV_B), lambda b, h: (0, 0))],
        out_specs=new,
        compiler_params=pltpu.CompilerParams(dimension_semantics=("arbitrary", "arbitrary"),
                                             vmem_limit_bytes=VMEM_LIMIT),
        name="attn_sample",
    )(lam, q, kn, vn, ck, cv, sw)


def _outproj_kernel(x_ref, oa_ref, ob_ref, w_ref, y_ref):
    y_ref[...] = (x_ref[...]
                  + jnp.dot(oa_ref[...], w_ref[0:W_A, :], preferred_element_type=F32)
                  + jnp.dot(ob_ref[...], w_ref[W_A:W_A + W_B, :], preferred_element_type=F32))


def _outproj(x2d, oa, ob, w_out, tm):
    T = x2d.shape[0]
    row = lambda c: pl.BlockSpec((tm, c), lambda i: (i, 0))
    return pl.pallas_call(
        _outproj_kernel,
        out_shape=jax.ShapeDtypeStruct((T, D_MODEL), F32),
        grid=(T // tm,),
        in_specs=[row(D_MODEL), row(W_A), row(W_B), pl.BlockSpec(w_out.shape, lambda i: (0, 0))],
        out_specs=row(D_MODEL),
        compiler_params=pltpu.CompilerParams(dimension_semantics=("arbitrary",), vmem_limit_bytes=VMEM_LIMIT),
        name="outproj",
    )(x2d, oa, ob, w_out)


def _top_values(s, k):
    n = s.shape[0]
    iota = lax.broadcasted_iota(jnp.int32, s.shape, 0)
    out = []
    for _ in range(k):
        m = jnp.max(s, axis=0, keepdims=True)
        out.append(m)
        first = jnp.min(jnp.where(s == m, iota, n), axis=0, keepdims=True)
        s = jnp.where(iota == first, -jnp.inf, s)
    return out


def _kth_largest(c, k):
    cnt = jnp.zeros((1, c.shape[1]), F32)
    kth = jnp.zeros((1, c.shape[1]), F32)
    for _ in range(k):
        m = jnp.max(c, axis=0, keepdims=True)
        eq = c == m
        new = cnt + jnp.sum(jnp.where(eq, 1.0, 0.0), axis=0, keepdims=True)
        kth = jnp.where((cnt < k) & (new >= k), m, kth)
        cnt = new
        c = jnp.where(eq, -jnp.inf, c)
    return kth


def _peer_score_kernel(x_ref, n2_ref, wq_ref, kblk_ref, ht_ref, s1_ref, e1_ref, s2_ref, e2_ref, tau_ref):
    x = x_ref[...]
    ms = jnp.mean(x * x, axis=-1, keepdims=True)
    h = x * lax.rsqrt(ms + EPS) * n2_ref[...]
    ht_ref[...] = h.T.astype(BF16)
    q = jnp.dot(h.astype(BF16), wq_ref[...], preferred_element_type=F32)
    st = lax.dot_general(kblk_ref[...], q.astype(BF16), NT_DIMS, preferred_element_type=F32)
    for hd in range(PEER_HEADS):
        s1 = st[(2 * hd) * N_KEYS:(2 * hd + 1) * N_KEYS, :]
        s2 = st[(2 * hd + 1) * N_KEYS:(2 * hd + 2) * N_KEYS, :]
        v1 = _top_values(s1, PEER_TOPK)
        v2 = _top_values(s2, PEER_TOPK)
        v2m = jnp.concatenate(v2, axis=0)
        cand = jnp.concatenate([v1[p] + v2m for p in range(PEER_TOPK)], axis=0)
        tau = _kth_largest(cand, PEER_TOPK)
        ev2 = jnp.exp(v2m - v2[0])
        w = jnp.concatenate([jnp.exp(v1[p] - v1[0]) * ev2 for p in range(PEER_TOPK)], axis=0)
        zsum = jnp.sum(jnp.where(cand >= tau, w, 0.0), axis=0, keepdims=True)
        s1_ref[hd] = s1
        s2_ref[hd] = s2
        e1_ref[hd] = jnp.exp(s1 - v1[0])
        e2_ref[hd] = jnp.exp(s2 - v2[0]) / zsum
        tau_ref[hd:hd + 1, :] = tau


def _peer_score(x1, n2, wq, kblk, tt):
    T = x1.shape[0]
    sshape = jax.ShapeDtypeStruct((PEER_HEADS, N_KEYS, T), F32)
    sspec = pl.BlockSpec((PEER_HEADS, N_KEYS, tt), lambda i: (0, 0, i))
    return pl.pallas_call(
        _peer_score_kernel,
        out_shape=(jax.ShapeDtypeStruct((D_MODEL, T), BF16), sshape, sshape, sshape, sshape,
                   jax.ShapeDtypeStruct((PEER_HEADS, T), F32)),
        grid=(T // tt,),
        in_specs=[pl.BlockSpec((tt, D_MODEL), lambda i: (i, 0)),
                  pl.BlockSpec(n2.shape, lambda i: (0, 0)),
                  pl.BlockSpec(wq.shape, lambda i: (0, 0)),
                  pl.BlockSpec(kblk.shape, lambda i: (0, 0))],
        out_specs=(pl.BlockSpec((D_MODEL, tt), lambda i: (0, i)), sspec, sspec, sspec, sspec,
                   pl.BlockSpec((PEER_HEADS, tt), lambda i: (0, i))),
        compiler_params=pltpu.CompilerParams(dimension_semantics=("arbitrary",), vmem_limit_bytes=VMEM_LIMIT),
        name="peer_score",
    )(x1, n2, wq, kblk)


def _peer_expert_kernel(ht_ref, u_ref, vt_ref, s1_ref, e1_ref, s2_ref, e2_ref, tau_ref, x_ref, nf_ref,
                        y_ref, acc_ref, ga_ref, hh_ref, *, na, tt):
    e = pl.program_id(1)

    @pl.when(e == 0)
    def _():
        acc_ref[...] = jnp.zeros(acc_ref.shape, F32)

    hh_ref[...] = jnp.dot(u_ref[...], ht_ref[...], preferred_element_type=F32)
    for a in range(na):
        rows = slice(a * N_KEYS, (a + 1) * N_KEYS)
        for ts in range(tt // LANES):
            cols = slice(ts * LANES, (ts + 1) * LANES)
            g = jnp.zeros((N_KEYS, LANES), F32)
            for hd in range(PEER_HEADS):
                s1row = s1_ref[hd, a:a + 1, cols]
                e1row = e1_ref[hd, a:a + 1, cols]
                sel = (s1row + s2_ref[hd, :, cols]) >= tau_ref[hd:hd + 1, cols]
                g = g + jnp.where(sel, e1row * e2_ref[hd, :, cols], 0.0)
            x = hh_ref[rows, cols]
            act = 0.5 * x * (1.0 + lax.erf(x * (2.0 ** -0.5)))
            ga_ref[rows, cols] = (g * act).astype(BF16)
    acc_ref[...] += jnp.dot(vt_ref[...], ga_ref[...], preferred_element_type=F32)

    @pl.when(e == pl.num_programs(1) - 1)
    def _():
        xo = x_ref[...] + acc_ref[...].T
        ms = jnp.mean(xo * xo, axis=-1, keepdims=True)
        y_ref[...] = xo * lax.rsqrt(ms + EPS) * nf_ref[...]


def _peer_expert(ht, u, vt, s1, e1, s2, e2, tau, x1, nf, tt, na):
    T = x1.shape[0]
    et = na * N_KEYS
    tok3 = pl.BlockSpec((PEER_HEADS, N_KEYS, tt), lambda i, e: (0, 0, i))
    exp3 = pl.BlockSpec((PEER_HEADS, na, tt), lambda i, e: (0, e, i))
    return pl.pallas_call(
        functools.partial(_peer_expert_kernel, na=na, tt=tt),
        out_shape=jax.ShapeDtypeStruct((T, D_MODEL), F32),
        grid=(T // tt, N_EXPERTS // et),
        in_specs=[pl.BlockSpec((D_MODEL, tt), lambda i, e: (0, i)),
                  pl.BlockSpec((et, D_MODEL), lambda i, e: (e, 0)),
                  pl.BlockSpec((D_MODEL, et), lambda i, e: (0, e)),
                  exp3, exp3, tok3, tok3,
                  pl.BlockSpec((PEER_HEADS, tt), lambda i, e: (0, i)),
                  pl.BlockSpec((tt, D_MODEL), lambda i, e: (i, 0)),
                  pl.BlockSpec(nf.shape, lambda i, e: (0, 0))],
        out_specs=pl.BlockSpec((tt, D_MODEL), lambda i, e: (i, 0)),
        scratch_shapes=[pltpu.VMEM((D_MODEL, tt), F32), pltpu.VMEM((et, tt), BF16), pltpu.VMEM((et, tt), F32)],
        compiler_params=pltpu.CompilerParams(dimension_semantics=("arbitrary", "arbitrary"),
                                             vmem_limit_bytes=VMEM_LIMIT),
        name="peer_expert",
    )(ht, u, vt, s1, e1, s2, e2, tau, x1, nf)


def _pick_tile(T, prefs):
    for t in prefs:
        if T % t == 0:
            return t
    raise ValueError(f"no tile for {T}")


def _layer(x, conv_prev, ssm_prev, k_cache, v_cache, lam, lam_init, wts):
    (n1, w_main, w_ba, w_bat, conv_w, arow, drow, acol, dcol, gnw, sw, w_out, n2, wq, kblk, u, vt, nf) = wts
    B, T, _ = x.shape
    n_tok = B * T
    x2d = x.reshape(n_tok, D_MODEL)
    tm = _pick_tile(n_tok, (256, 128))
    conv_in, z, qb, kb, vb, kbh, vbh, ba, bat = _inproj(x2d, n1, w_main, w_ba, w_bat, tm)

    C = min(CHUNK, T)
    prev8 = jnp.pad(conv_prev.astype(F32), ((0, 0), (SUBLANES - CONV_W + 1, 0), (0, 0)))
    bat4 = bat.reshape(SUBLANES, B, T // C, C).transpose(1, 2, 0, 3)
    o_a, ssm_new = _gdn(conv_in.reshape(B, T, CONV_CH), z.reshape(B, T, W_A), ba.reshape(B, T, LANES), bat4,
                        prev8, ssm_prev.astype(F32), conv_w, arow, drow, acol, dcol, gnw, C)
    conv_new = conv_in.reshape(B, T, CONV_CH)[:, T - (CONV_W - 1):, :]

    if k_cache is None:
        tq = _pick_tile(T, (512, 256, 128))
        o_b = _attn_prompt(lam, qb, kbh, vbh, sw, tq, lam_init)
    else:
        P = k_cache.shape[1]
        o_b = _attn_sample(lam, qb.reshape(B, T, W_B), kbh.reshape(B, T, W_B), vbh.reshape(B, T, W_B),
                           k_cache.reshape(B, P, W_B), v_cache.reshape(B, P, W_B), sw, lam_init)
        o_b = o_b.reshape(n_tok, W_B)

    x1 = _outproj(x2d, o_a.reshape(n_tok, W_A), o_b, w_out, tm)

    tt = _pick_tile(n_tok, (256, 128))
    ht, s1, e1, s2, e2, tau = _peer_score(x1, n2, wq, kblk, tt)
    tt2 = _pick_tile(n_tok, (512, 256, 128))
    y = _peer_expert(ht, u, vt, s1, e1, s2, e2, tau, x1, nf, tt2, SUBLANES)
    return (y.reshape(B, T, D_MODEL), kb.reshape(B, T, H_B, 2 * DH_B), vb.reshape(B, T, H_B, DV_B),
            conv_new, ssm_new)


def _prep_weights(norm1_w, w_in, conv_w, a_log, dt_bias, gdn_norm_w, subln_w, w_out,
                  norm2_w, peer_wq, peer_k1, peer_k2, peer_u, peer_v, norm_f_w):
    o_beta = CONV_CH + W_A
    w_main = jnp.concatenate([w_in[:, :o_beta], w_in[:, o_beta + 2 * H_A:]], axis=1).astype(BF16)
    w_gate = w_in[:, o_beta:o_beta + 2 * H_A]
    w_ba = jnp.pad(w_gate, ((0, 0), (0, LANES - 2 * H_A))).astype(BF16)
    w_bat = w_gate.T.astype(BF16)
    zeros4 = jnp.zeros((H_A,), F32)
    a8 = jnp.concatenate([zeros4, a_log.astype(F32)])
    d8 = jnp.concatenate([zeros4, dt_bias.astype(F32)])
    arow = jnp.pad(a8, (0, LANES - 2 * H_A)).reshape(1, LANES)
    drow = jnp.pad(d8, (0, LANES - 2 * H_A)).reshape(1, LANES)
    acol = a8.reshape(2 * H_A, 1)
    dcol = d8.reshape(2 * H_A, 1)
    halves = jnp.stack([peer_k1, peer_k2]).astype(F32)
    eye = jnp.eye(2 * PEER_HEADS, dtype=F32)
    kblk = jnp.einsum("gj,gnd->gnjd", eye, jnp.tile(halves, (PEER_HEADS, 1, 1)))
    kblk = kblk.reshape(2 * PEER_HEADS * N_KEYS, D_MODEL).astype(BF16)
    return (norm1_w.reshape(1, D_MODEL), w_main, w_ba, w_bat, conv_w, arow, drow, acol, dcol,
            gdn_norm_w.reshape(1, DK_A), subln_w.reshape(1, DV_B), w_out.astype(BF16),
            norm2_w.reshape(1, D_MODEL), peer_wq.astype(BF16), kblk, peer_u.astype(BF16),
            peer_v.T.astype(BF16), norm_f_w.reshape(1, D_MODEL))


def kernel(x_prompt, x_sample, cache_k, cache_v, state_conv, state_gdn, norm1_w, w_in, conv_w, a_log, dt_bias, gdn_norm_w, lam_q1, lam_k1, lam_q2, lam_k2, subln_w, w_out, norm2_w, peer_wq, peer_k1, peer_k2, peer_u, peer_v, norm_f_w):
    depth = w_in.shape[0]
    assert depth == 1, "the final norm is fused into the (single) layer's last kernel"
    l = 0
    lam_init = 0.8 - 0.6 * math.exp(-0.3 * l)
    lam = (jnp.exp(jnp.sum(lam_q1[l].astype(F32) * lam_k1[l].astype(F32)))
           - jnp.exp(jnp.sum(lam_q2[l].astype(F32) * lam_k2[l].astype(F32))) + lam_init).reshape(1)
    wts = _prep_weights(norm1_w[l], w_in[l], conv_w[l], a_log[l], dt_bias[l], gdn_norm_w[l], subln_w[l],
                        w_out[l], norm2_w[l], peer_wq[l], peer_k1[l], peer_k2[l], peer_u[l], peer_v[l], norm_f_w)
    bp = x_prompt.shape[0]
    conv0 = jnp.zeros((bp, CONV_W - 1, CONV_CH), F32)
    ssm0 = jnp.zeros((bp, H_A, DK_A, DK_A), F32)
    yp, k1, v1, c1, s1 = _layer(x_prompt, conv0, ssm0, None, None, lam, lam_init, wts)
    ys, k2, v2, c2, s2 = _layer(x_sample, state_conv[l], state_gdn[l], cache_k[l], cache_v[l], lam, lam_init, wts)
    st = lambda a: a[None]
    return (yp, ys, st(k1), st(v1), st(c1), st(s1), st(k2), st(v2), st(c2), st(s2))
```

```python
import functools
import math

import jax
import jax.numpy as jnp
from jax import lax
from jax.experimental import pallas as pl
from jax.experimental.pallas import tpu as pltpu

F32 = jnp.float32
BF16 = jnp.bfloat16
HIGHEST = lax.Precision.HIGHEST

D_MODEL = 1024
CHUNK = 64
H_A = 4
DK_A = 128
CONV_W = 4
W_A = H_A * DK_A
CONV_CH = 3 * W_A
H_B = 4
DH_B = 64
DV_B = 128
W_B = H_B * DV_B
DIFF_EPS = 1e-5
N_KEYS = 128
N_EXPERTS = N_KEYS * N_KEYS
PEER_HEADS = 8
PEER_TOPK = 16
PEER_DK_HALF = 64
EPS = 1e-6
LANES = 128
SUBLANES = 8
VMEM_LIMIT = 56 * 1024 * 1024

NT_DIMS = (((1,), (1,)), ((), ()))
TN_DIMS = (((0,), (0,)), ((), ()))


def _sigmoid(x):
    return 1.0 / (1.0 + jnp.exp(-x))


def _softplus(x):
    return jnp.maximum(x, 0.0) + jnp.log(1.0 + jnp.exp(-jnp.abs(x)))


def _dot_hi(a, b):
    return jnp.dot(a, b, precision=HIGHEST, preferred_element_type=F32)


def _dot_bf(a, b, dims=(((1,), (0,)), ((), ()))):
    return lax.dot_general(a.astype(BF16), b.astype(BF16), dims, preferred_element_type=F32)


def _split_bf(a):
    hi = a.astype(BF16)
    return hi, (a - hi.astype(F32)).astype(BF16)


def _dot_split(a, b):
    d = lambda x, y: jnp.dot(x, y, preferred_element_type=F32)
    return d(a[0], b[0]) + (d(a[0], b[1]) + d(a[1], b[0]))


def _inproj_kernel(x_ref, n1_ref, w_ref, wba_ref, wbat_ref,
                   conv_ref, z_ref, qb_ref, kb_ref, vb_ref, kbh_ref, vbh_ref, ba_ref, bat_ref):
    x = x_ref[...]
    ms = jnp.mean(x * x, axis=-1, keepdims=True)
    h = (x * lax.rsqrt(ms + EPS) * n1_ref[...]).astype(BF16)

    def mm(c0, c1):
        return jnp.dot(h, w_ref[:, c0:c1], preferred_element_type=F32)

    conv_ref[...] = mm(0, CONV_CH)
    z_ref[...] = mm(CONV_CH, CONV_CH + W_A)
    o = CONV_CH + W_A
    qb_ref[...] = (mm(o, o + W_B) * (DH_B ** -0.5)).astype(BF16)
    kb = mm(o + W_B, o + 2 * W_B)
    kb_ref[...] = kb
    kbh_ref[...] = kb.astype(BF16)
    vb = mm(o + 2 * W_B, o + 3 * W_B)
    vb_ref[...] = vb
    vbh_ref[...] = vb.astype(BF16)
    ba_ref[...] = jnp.dot(h, wba_ref[...], preferred_element_type=F32)
    bat_ref[...] = lax.dot_general(wbat_ref[...], h, NT_DIMS, preferred_element_type=F32)


def _inproj(x2d, n1, w_main, w_ba, w_bat, tm):
    T = x2d.shape[0]
    row = lambda c: pl.BlockSpec((tm, c), lambda i: (i, 0))
    full = lambda a: pl.BlockSpec(a.shape, lambda i: (0, 0))
    out_shape = (
        jax.ShapeDtypeStruct((T, CONV_CH), F32),
        jax.ShapeDtypeStruct((T, W_A), F32),
        jax.ShapeDtypeStruct((T, W_B), BF16),
        jax.ShapeDtypeStruct((T, W_B), F32),
        jax.ShapeDtypeStruct((T, W_B), F32),
        jax.ShapeDtypeStruct((T, W_B), BF16),
        jax.ShapeDtypeStruct((T, W_B), BF16),
        jax.ShapeDtypeStruct((T, LANES), F32),
        jax.ShapeDtypeStruct((SUBLANES, T), F32),
    )
    out_specs = (row(CONV_CH), row(W_A), row(W_B), row(W_B), row(W_B), row(W_B), row(W_B), row(LANES),
                 pl.BlockSpec((SUBLANES, tm), lambda i: (0, i)))
    return pl.pallas_call(
        _inproj_kernel,
        out_shape=out_shape,
        grid=(T // tm,),
        in_specs=[row(D_MODEL), full(n1), full(w_main), full(w_ba), full(w_bat)],
        out_specs=out_specs,
        compiler_params=pltpu.CompilerParams(dimension_semantics=("arbitrary",), vmem_limit_bytes=VMEM_LIMIT),
        name="inproj",
    )(x2d, n1, w_main, w_ba, w_bat)


def _gdn_kernel(conv_ref, z_ref, ba_ref, bat_ref, prev_ref, s0_ref, cw_ref, arow_ref, drow_ref,
                acol_ref, dcol_ref, nw_ref, o_ref, sout_ref, up_ref, s_ref, *, C):
    t = pl.program_id(1)

    @pl.when(t == 0)
    def _():
        up_ref[0:SUBLANES, :] = prev_ref[...]
        s_ref[...] = s0_ref[...]

    up_ref[SUBLANES:SUBLANES + C, :] = conv_ref[...]
    y = up_ref[pl.ds(SUBLANES - CONV_W + 1, C), :] * cw_ref[0:1, :]
    for j in range(1, CONV_W):
        y = y + up_ref[pl.ds(SUBLANES - CONV_W + 1 + j, C), :] * cw_ref[j:j + 1, :]
    qkv = y * _sigmoid(y)
    tail = up_ref[C:C + SUBLANES, :]
    up_ref[0:SUBLANES, :] = tail

    ri = lax.broadcasted_iota(jnp.int32, (C, C), 0)
    ci = lax.broadcasted_iota(jnp.int32, (C, C), 1)
    incl = ri >= ci
    strict = ri > ci
    tril = incl.astype(F32)
    triu = (ri <= ci).astype(F32)
    eye = (ri == ci).astype(F32)

    ba = ba_ref[...]
    beta_all = _sigmoid(ba)
    g_all = -jnp.exp(arow_ref[...]) * _softplus(ba + drow_ref[...])
    gc_all = _dot_hi(tril, g_all)
    bat = bat_ref[...]
    g_t = -jnp.exp(acol_ref[...]) * _softplus(bat + dcol_ref[...])
    gc_t = _dot_hi(g_t, triu)

    nw = nw_ref[...]
    for h in range(H_A):
        gcc = gc_all[:, H_A + h:H_A + h + 1]
        gcr = gc_t[H_A + h:H_A + h + 1, :]
        beta = beta_all[:, h:h + 1]
        decay = jnp.where(incl, jnp.exp(jnp.where(incl, gcc - gcr, 0.0)), 0.0)
        qh = qkv[:, h * DK_A:(h + 1) * DK_A]
        kh = qkv[:, W_A + h * DK_A:W_A + (h + 1) * DK_A]
        vh = qkv[:, 2 * W_A + h * DK_A:2 * W_A + (h + 1) * DK_A]
        qh = qh * lax.rsqrt(jnp.sum(qh * qh, axis=-1, keepdims=True) + EPS) * (DK_A ** -0.5)
        kh = kh * lax.rsqrt(jnp.sum(kh * kh, axis=-1, keepdims=True) + EPS)
        kb = kh * beta
        a = _dot_bf(kb, kh, NT_DIMS)
        n = jnp.where(strict, -(a * decay), 0.0)
        tinv = eye + n
        p = _split_bf(n)
        for _ in range(int(math.log2(C)) - 1):
            p = _split_bf(_dot_split(p, p))
            tinv = tinv + _dot_split(_split_bf(tinv), p)
        egc = jnp.exp(gcc)
        tinv_bf = tinv.astype(BF16)
        u = _dot_bf(tinv_bf, vh * beta)
        w = _dot_bf(tinv_bf, kb * egc)
        attn = _dot_bf(qh, kh, NT_DIMS) * decay
        s = s_ref[h]
        s_bf = s.astype(BF16)
        v_new = u - _dot_bf(w, s_bf)
        o = _dot_bf(qh * egc, s_bf) + _dot_bf(attn, v_new)
        g_last = gcc[C - 1:C, :]
        kd = kh * jnp.exp(g_last - gcc)
        s_ref[h] = s * jnp.exp(g_last) + _dot_bf(kd, v_new, TN_DIMS)
        on = o * lax.rsqrt(jnp.mean(o * o, axis=-1, keepdims=True) + EPS) * nw
        zz = z_ref[:, h * DK_A:(h + 1) * DK_A]
        o_ref[:, h * DK_A:(h + 1) * DK_A] = (on * (zz * _sigmoid(zz))).astype(o_ref.dtype)

    @pl.when(t == pl.num_programs(1) - 1)
    def _():
        sout_ref[...] = s_ref[...]


def _gdn(conv_in, z, ba, bat, prev8, s0, conv_w, arow, drow, acol, dcol, nw, C):
    B, T, _ = conv_in.shape
    tok = lambda c: pl.BlockSpec((None, C, c), lambda b, t: (b, t, 0))
    full2 = lambda a: pl.BlockSpec(a.shape, lambda b, t: (0, 0))
    return pl.pallas_call(
        functools.partial(_gdn_kernel, C=C),
        out_shape=(jax.ShapeDtypeStruct((B, T, W_A), BF16),
                   jax.ShapeDtypeStruct((B, H_A, DK_A, DK_A), F32)),
        grid=(B, T // C),
        in_specs=[tok(CONV_CH), tok(W_A), tok(LANES),
                  pl.BlockSpec((None, None, SUBLANES, C), lambda b, t: (b, t, 0, 0)),
                  pl.BlockSpec((None, SUBLANES, CONV_CH), lambda b, t: (b, 0, 0)),
                  pl.BlockSpec((None, H_A, DK_A, DK_A), lambda b, t: (b, 0, 0, 0)),
                  full2(conv_w), full2(arow), full2(drow), full2(acol), full2(dcol), full2(nw)],
        out_specs=(tok(W_A), pl.BlockSpec((None, H_A, DK_A, DK_A), lambda b, t: (b, 0, 0, 0))),
        scratch_shapes=[pltpu.VMEM((SUBLANES + C, CONV_CH), F32), pltpu.VMEM((H_A, DK_A, DK_A), F32)],
        compiler_params=pltpu.CompilerParams(dimension_semantics=("arbitrary", "arbitrary"),
                                             vmem_limit_bytes=VMEM_LIMIT),
        name="gdn",
    )(conv_in, z, ba, bat, prev8, s0, conv_w, arow, drow, acol, dcol, nw)


def _split_maps(q):
    lane = lax.broadcasted_iota(jnp.int32, q.shape, 1)
    zero = jnp.zeros_like(q)
    return jnp.concatenate([jnp.where(lane < DH_B, q, zero), jnp.where(lane >= DH_B, q, zero)], axis=0)


def _subln(acc, l, lam, sw, tq, lam_init):
    o = acc[:tq] / l[:tq] - lam * (acc[tq:] / l[tq:])
    return o * lax.rsqrt(jnp.mean(o * o, axis=-1, keepdims=True) + DIFF_EPS) * sw * (1.0 - lam_init)


def _attn_prompt_kernel(lam_ref, q_ref, k_ref, v_ref, sw_ref, o_ref, qs_ref, s_ref, m_ref, l_ref, acc_ref,
                        *, tq, rc, lam_init):
    qi = pl.program_id(1)
    qs_ref[...] = _split_maps(q_ref[...])
    m_ref[...] = jnp.full(m_ref.shape, -jnp.inf, F32)
    l_ref[...] = jnp.zeros(l_ref.shape, F32)
    acc_ref[...] = jnp.zeros(acc_ref.shape, F32)
    chunks = [slice(c * rc, (c + 1) * rc) for c in range(2 * tq // rc)]
    rep = tq // LANES

    def scores(j):
        kblk = k_ref[pl.ds(pl.multiple_of(j * tq, tq), tq), :]
        for rows in chunks:
            s_ref[rows, :] = lax.dot_general(qs_ref[rows, :], kblk, NT_DIMS, preferred_element_type=F32)

    def softmax_pv(j, masked):
        vblk = v_ref[pl.ds(pl.multiple_of(j * tq, tq), tq), :]
        for rows in chunks:
            s = s_ref[rows, :]
            if masked:
                r = lax.broadcasted_iota(jnp.int32, s.shape, 0) + (rows.start % tq)
                c = lax.broadcasted_iota(jnp.int32, s.shape, 1)
                s = jnp.where((c // CHUNK) <= (r // CHUNK), s, -jnp.inf)
            m_old = m_ref[rows, :]
            m_new = jnp.maximum(m_old, jnp.max(s, axis=-1, keepdims=True))
            alpha = jnp.exp(m_old - m_new)
            p = jnp.exp(s - jnp.concatenate([m_new] * rep, axis=1))
            l_ref[rows, :] = alpha * l_ref[rows, :] + jnp.sum(p, axis=-1, keepdims=True)
            acc_ref[rows, :] = alpha * acc_ref[rows, :] + jnp.dot(p.astype(BF16), vblk,
                                                                  preferred_element_type=F32)
            m_ref[rows, :] = m_new

    scores(qi)
    softmax_pv(qi, True)

    @pl.when(qi > 0)
    def _():
        scores(0)

        def body(j, carry):
            softmax_pv(j, False)
            scores(j + 1)
            return carry

        lax.fori_loop(0, qi - 1, body, 0)
        softmax_pv(qi - 1, False)

    o_ref[...] = _subln(acc_ref[...], l_ref[...], lam_ref[0], sw_ref[...], tq, lam_init).astype(o_ref.dtype)


def _attn_prompt(lam, q, k, v, sw, tq, lam_init):
    T = q.shape[0]
    return pl.pallas_call(
        functools.partial(_attn_prompt_kernel, tq=tq, rc=min(256, tq), lam_init=lam_init),
        out_shape=jax.ShapeDtypeStruct((T, W_B), BF16),
        grid=(H_B, T // tq),
        in_specs=[pl.BlockSpec(memory_space=pltpu.SMEM),
                  pl.BlockSpec((tq, DV_B), lambda h, i: (i, h)),
                  pl.BlockSpec((T, DV_B), lambda h, i: (0, h)),
                  pl.BlockSpec((T, DV_B), lambda h, i: (0, h)),
                  pl.BlockSpec((1, DV_B), lambda h, i: (0, 0))],
        out_specs=pl.BlockSpec((tq, DV_B), lambda h, i: (i, h)),
        scratch_shapes=[pltpu.VMEM((2 * tq, DV_B), BF16), pltpu.VMEM((2 * tq, tq), F32),
                        pltpu.VMEM((2 * tq, LANES), F32), pltpu.VMEM((2 * tq, LANES), F32),
                        pltpu.VMEM((2 * tq, DV_B), F32)],
        compiler_params=pltpu.CompilerParams(dimension_semantics=("arbitrary", "arbitrary"),
                                             vmem_limit_bytes=VMEM_LIMIT),
        name="attn_prompt",
    )(lam, q, k, v, sw)


def _attn_sample_kernel(lam_ref, q_ref, kn_ref, vn_ref, ck_ref, cv_ref, sw_ref, o_ref, *, tq, lam_init):
    qs = _split_maps(q_ref[...])
    kc = ck_ref[...].astype(BF16)
    vc = cv_ref[...].astype(BF16)
    sc = lax.dot_general(qs, kc, NT_DIMS, preferred_element_type=F32)
    sn = lax.dot_general(qs, kn_ref[...], NT_DIMS, preferred_element_type=F32)
    m = jnp.maximum(jnp.max(sc, axis=-1, keepdims=True), jnp.max(sn, axis=-1, keepdims=True))
    pc = jnp.exp(sc - m)
    pn = jnp.exp(sn - m)
    l = jnp.sum(pc, axis=-1, keepdims=True) + jnp.sum(pn, axis=-1, keepdims=True)
    acc = (jnp.dot(pc.astype(BF16), vc, preferred_element_type=F32)
           + jnp.dot(pn.astype(BF16), vn_ref[...], preferred_element_type=F32))
    o_ref[...] = _subln(acc, l, lam_ref[0], sw_ref[...], tq, lam_init).astype(o_ref.dtype)


def _attn_sample(lam, q, kn, vn, ck, cv, sw, lam_init):
    B, tq, _ = q.shape
    P = ck.shape[1]
    new = pl.BlockSpec((None, tq, DV_B), lambda b, h: (b, 0, h))
    cache = pl.BlockSpec((None, P, DV_B), lambda b, h: (b, 0, h))
    return pl.pallas_call(
        functools.partial(_attn_sample_kernel, tq=tq, lam_init=lam_init),
        out_shape=jax.ShapeDtypeStruct((B, tq, W_B), BF16),
        grid=(B, H_B),
        in_specs=[pl.BlockSpec(memory_space=pltpu.SMEM), new, new, new, cache, cache,
                  pl.BlockSpec((1, DV_B), lambda b, h: (0, 0))],
        out_specs=new,
        compiler_params=pltpu.CompilerParams(dimension_semantics=("arbitrary", "arbitrary"),
                                             vmem_limit_bytes=VMEM_LIMIT),
        name="attn_sample",
    )(lam, q, kn, vn, ck, cv, sw)


def _outproj_kernel(x_ref, oa_ref, ob_ref, w_ref, y_ref):
    y_ref[...] = (x_ref[...]
                  + jnp.dot(oa_ref[...], w_ref[0:W_A, :], preferred_element_type=F32)
                  + jnp.dot(ob_ref[...], w_ref[W_A:W_A + W_B, :], preferred_element_type=F32))


def _outproj(x2d, oa, ob, w_out, tm):
    T = x2d.shape[0]
    row = lambda c: pl.BlockSpec((tm, c), lambda i: (i, 0))
    return pl.pallas_call(
        _outproj_kernel,
        out_shape=jax.ShapeDtypeStruct((T, D_MODEL), F32),
        grid=(T // tm,),
        in_specs=[row(D_MODEL), row(W_A), row(W_B), pl.BlockSpec(w_out.shape, lambda i: (0, 0))],
        out_specs=row(D_MODEL),
        compiler_params=pltpu.CompilerParams(dimension_semantics=("arbitrary",), vmem_limit_bytes=VMEM_LIMIT),
        name="outproj",
    )(x2d, oa, ob, w_out)


_NEG_INF = float("-inf")
_PAIR_SLAB = SUBLANES


def _top16_ranked(s):
    n = s.shape[0]
    iota = lax.broadcasted_iota(jnp.int32, s.shape, 0)
    rank = jnp.full(s.shape, float(PEER_TOPK), F32)
    vals = []
    for p in range(PEER_TOPK):
        m = jnp.max(s, axis=0, keepdims=True)
        first = jnp.min(jnp.where(s == m, iota, n), axis=0, keepdims=True)
        hit = iota == first
        s = jnp.where(hit, _NEG_INF, s)
        rank = jnp.where(hit, float(p), rank)
        vals.append(m)
    return vals, rank


def _pair_counts(v1, v2):
    t = v1[0].shape[1]
    v1m = jnp.concatenate(v1, axis=0)
    v2m = jnp.concatenate(v2, axis=0)
    v2s = v2m[0:_PAIR_SLAB]
    i16 = lax.broadcasted_iota(jnp.int32, (PEER_TOPK, t), 0)
    i8 = lax.broadcasted_iota(jnp.int32, (_PAIR_SLAB, t), 0)
    slabs = [v1[0] + v2m, v1[1] + v2s]
    flats = [i16, PEER_TOPK + i8]
    for p in range(2, _PAIR_SLAB):
        slabs.append(jnp.where(i8 < PEER_TOPK // (p + 1), v1[p] + v2s, _NEG_INF))
        flats.append(PEER_TOPK * p + i8)
    slabs.append(v1m[_PAIR_SLAB:] + v2[0])
    flats.append(PEER_TOPK * (_PAIR_SLAB + i8))
    cand = jnp.concatenate(slabs, axis=0)
    flat = jnp.concatenate(flats, axis=0)
    big = PEER_TOPK * PEER_TOPK
    c = cand
    for _ in range(PEER_TOPK):
        m = jnp.max(c, axis=0, keepdims=True)
        first = jnp.min(jnp.where(c == m, flat, big), axis=0, keepdims=True)
        c = jnp.where(flat == first, _NEG_INF, c)
    sel = c != cand
    self = jnp.where(sel, 1.0, 0.0)
    counts = [jnp.sum(self[0:PEER_TOPK], axis=0, keepdims=True)]
    for p in range(1, _PAIR_SLAB):
        r0 = PEER_TOPK + (p - 1) * _PAIR_SLAB
        counts.append(jnp.sum(self[r0:r0 + _PAIR_SLAB], axis=0, keepdims=True))
    r0 = PEER_TOPK + (_PAIR_SLAB - 1) * _PAIR_SLAB
    for r in range(_PAIR_SLAB):
        counts.append(self[r0 + r:r0 + r + 1])
    zsum = jnp.sum(jnp.where(sel, jnp.exp(cand - cand[0:1]), 0.0), axis=0, keepdims=True)
    return counts, zsum


def _peer_score_kernel(x_ref, n2_ref, wq_ref, kblk_ref, ht_ref, c1_ref, e1_ref, r2_ref, e2_ref, st_ref, *, tt):
    x = x_ref[...]
    ms = jnp.mean(x * x, axis=-1, keepdims=True)
    h = x * lax.rsqrt(ms + EPS) * n2_ref[...]
    ht_ref[...] = h.T.astype(BF16)
    q = jnp.dot(h.astype(BF16), wq_ref[...], preferred_element_type=F32)
    st_ref[...] = lax.dot_general(kblk_ref[...], q.astype(BF16), NT_DIMS, preferred_element_type=F32)

    def head(hd, carry):
        r1 = pl.multiple_of(hd * (2 * N_KEYS), 2 * N_KEYS)
        for g in range(tt // LANES):
            cols = slice(g * LANES, (g + 1) * LANES)
            s1 = st_ref[pl.ds(r1, N_KEYS), cols]
            s2 = st_ref[pl.ds(r1 + N_KEYS, N_KEYS), cols]
            v1, rank1 = _top16_ranked(s1)
            v2, rank2 = _top16_ranked(s2)
            counts, zsum = _pair_counts(v1, v2)
            c1 = jnp.zeros(s1.shape, F32)
            for p in range(PEER_TOPK):
                c1 = jnp.where(rank1 == float(p), counts[p], c1)
            c1_ref[hd, :, cols] = c1
            e1_ref[hd, :, cols] = jnp.exp(s1 - v1[0])
            r2_ref[hd, :, cols] = rank2
            e2_ref[hd, :, cols] = jnp.exp(s2 - v2[0]) * (1.0 / zsum)
        return carry

    lax.fori_loop(0, PEER_HEADS, head, 0)


def _peer_score(x1, n2, wq, kblk, tt):
    T = x1.shape[0]
    spec = pl.BlockSpec((PEER_HEADS, N_KEYS, tt), lambda i: (0, 0, i))
    f32s = jax.ShapeDtypeStruct((PEER_HEADS, N_KEYS, T), F32)
    return pl.pallas_call(
        functools.partial(_peer_score_kernel, tt=tt),
        out_shape=(jax.ShapeDtypeStruct((D_MODEL, T), BF16), f32s, f32s, f32s, f32s),
        grid=(T // tt,),
        in_specs=[pl.BlockSpec((tt, D_MODEL), lambda i: (i, 0)),
                  pl.BlockSpec(n2.shape, lambda i: (0, 0)),
                  pl.BlockSpec(wq.shape, lambda i: (0, 0)),
                  pl.BlockSpec(kblk.shape, lambda i: (0, 0))],
        out_specs=(pl.BlockSpec((D_MODEL, tt), lambda i: (0, i)), spec, spec, spec, spec),
        scratch_shapes=[pltpu.VMEM((2 * PEER_HEADS * N_KEYS, tt), F32)],
        compiler_params=pltpu.CompilerParams(dimension_semantics=("arbitrary",), vmem_limit_bytes=VMEM_LIMIT),
        name="peer_score",
    )(x1, n2, wq, kblk)


def _peer_expert_kernel(ht_ref, u_ref, vt_ref, c1_ref, e1_ref, r2_ref, e2_ref, x_ref, nf_ref,
                        y_ref, acc_ref, ga_ref, hh_ref, r2b_ref, e2b_ref, *, na, tt):
    e = pl.program_id(1)

    @pl.when(e == 0)
    def _():
        acc_ref[...] = jnp.zeros(acc_ref.shape, F32)
        r2b_ref[...] = r2_ref[...].astype(BF16)
        e2b_ref[...] = e2_ref[...].astype(BF16)

    hh_ref[...] = jnp.dot(u_ref[...], ht_ref[...], preferred_element_type=F32)
    pack = 2 * SUBLANES
    zero = jnp.zeros((N_KEYS, LANES), BF16)
    for a in range(na):
        rows = slice(a * N_KEYS, (a + 1) * N_KEYS)
        for ts in range(tt // LANES):
            cols = slice(ts * LANES, (ts + 1) * LANES)
            g = zero
            for hd in range(PEER_HEADS):
                c1row = jnp.broadcast_to(c1_ref[hd, a:a + 1, cols], (pack, LANES)).astype(BF16)
                e1row = jnp.broadcast_to(e1_ref[hd, a:a + 1, cols], (pack, LANES)).astype(BF16)
                c1b = jnp.concatenate([c1row] * (N_KEYS // pack), axis=0)
                e1b = jnp.concatenate([e1row] * (N_KEYS // pack), axis=0)
                sel = r2b_ref[hd, :, cols] < c1b
                g = g + jnp.where(sel, e1b * e2b_ref[hd, :, cols], zero)
            x = hh_ref[rows, cols]
            act = 0.5 * x * (1.0 + lax.erf(x * (2.0 ** -0.5)))
            ga_ref[rows, cols] = g * act.astype(BF16)
    acc_ref[...] += jnp.dot(vt_ref[...], ga_ref[...], preferred_element_type=F32)

    @pl.when(e == pl.num_programs(1) - 1)
    def _():
        xo = x_ref[...] + acc_ref[...].T
        ms = jnp.mean(xo * xo, axis=-1, keepdims=True)
        y_ref[...] = xo * lax.rsqrt(ms + EPS) * nf_ref[...]


def _peer_expert(ht, u, vt, c1, e1, r2, e2, x1, nf, tt, na):
    T = x1.shape[0]
    et = na * N_KEYS
    tok3 = pl.BlockSpec((PEER_HEADS, N_KEYS, tt), lambda i, e: (0, 0, i))
    exp3 = pl.BlockSpec((PEER_HEADS, na, tt), lambda i, e: (0, e, i))
    return pl.pallas_call(
        functools.partial(_peer_expert_kernel, na=na, tt=tt),
        out_shape=jax.ShapeDtypeStruct((T, D_MODEL), F32),
        grid=(T // tt, N_EXPERTS // et),
        in_specs=[pl.BlockSpec((D_MODEL, tt), lambda i, e: (0, i)),
                  pl.BlockSpec((et, D_MODEL), lambda i, e: (e, 0)),
                  pl.BlockSpec((D_MODEL, et), lambda i, e: (0, e)),
                  exp3, exp3, tok3, tok3,
                  pl.BlockSpec((tt, D_MODEL), lambda i, e: (i, 0)),
                  pl.BlockSpec(nf.shape, lambda i, e: (0, 0))],
        out_specs=pl.BlockSpec((tt, D_MODEL), lambda i, e: (i, 0)),
        scratch_shapes=[pltpu.VMEM((D_MODEL, tt), F32), pltpu.VMEM((et, tt), BF16), pltpu.VMEM((et, tt), F32),
                        pltpu.VMEM((PEER_HEADS, N_KEYS, tt), BF16), pltpu.VMEM((PEER_HEADS, N_KEYS, tt), BF16)],
        compiler_params=pltpu.CompilerParams(dimension_semantics=("arbitrary", "arbitrary"),
                                             vmem_limit_bytes=VMEM_LIMIT),
        name="peer_expert",
    )(ht, u, vt, c1, e1, r2, e2, x1, nf)


def _pick_tile(T, prefs):
    for t in prefs:
        if T % t == 0:
            return t
    raise ValueError(f"no tile for {T}")


def _layer(x, conv_prev, ssm_prev, k_cache, v_cache, lam, lam_init, wts):
    (n1, w_main, w_ba, w_bat, conv_w, arow, drow, acol, dcol, gnw, sw, w_out, n2, wq, kblk, u, vt, nf) = wts
    B, T, _ = x.shape
    n_tok = B * T
    x2d = x.reshape(n_tok, D_MODEL)
    tm = _pick_tile(n_tok, (256, 128))
    conv_in, z, qb, kb, vb, kbh, vbh, ba, bat = _inproj(x2d, n1, w_main, w_ba, w_bat, tm)

    C = min(CHUNK, T)
    prev8 = jnp.pad(conv_prev.astype(F32), ((0, 0), (SUBLANES - CONV_W + 1, 0), (0, 0)))
    bat4 = bat.reshape(SUBLANES, B, T // C, C).transpose(1, 2, 0, 3)
    o_a, ssm_new = _gdn(conv_in.reshape(B, T, CONV_CH), z.reshape(B, T, W_A), ba.reshape(B, T, LANES), bat4,
                        prev8, ssm_prev.astype(F32), conv_w, arow, drow, acol, dcol, gnw, C)
    conv_new = conv_in.reshape(B, T, CONV_CH)[:, T - (CONV_W - 1):, :]

    if k_cache is None:
        tq = _pick_tile(T, (512, 256, 128))
        o_b = _attn_prompt(lam, qb, kbh, vbh, sw, tq, lam_init)
    else:
        P = k_cache.shape[1]
        o_b = _attn_sample(lam, qb.reshape(B, T, W_B), kbh.reshape(B, T, W_B), vbh.reshape(B, T, W_B),
                           k_cache.reshape(B, P, W_B), v_cache.reshape(B, P, W_B), sw, lam_init)
        o_b = o_b.reshape(n_tok, W_B)

    x1 = _outproj(x2d, o_a.reshape(n_tok, W_A), o_b, w_out, tm)

    tt = _pick_tile(n_tok, (256, 128))
    ht, c1, e1, r2, e2 = _peer_score(x1, n2, wq, kblk, tt)
    tt2 = _pick_tile(n_tok, (512, 256, 128))
    y = _peer_expert(ht, u, vt, c1, e1, r2, e2, x1, nf, tt2, SUBLANES)
    return (y.reshape(B, T, D_MODEL), kb.reshape(B, T, H_B, 2 * DH_B), vb.reshape(B, T, H_B, DV_B),
            conv_new, ssm_new)


def _prep_weights(norm1_w, w_in, conv_w, a_log, dt_bias, gdn_norm_w, subln_w, w_out,
                  norm2_w, peer_wq, peer_k1, peer_k2, peer_u, peer_v, norm_f_w):
    o_beta = CONV_CH + W_A
    w_main = jnp.concatenate([w_in[:, :o_beta], w_in[:, o_beta + 2 * H_A:]], axis=1).astype(BF16)
    w_gate = w_in[:, o_beta:o_beta + 2 * H_A]
    w_ba = jnp.pad(w_gate, ((0, 0), (0, LANES - 2 * H_A))).astype(BF16)
    w_bat = w_gate.T.astype(BF16)
    zeros4 = jnp.zeros((H_A,), F32)
    a8 = jnp.concatenate([zeros4, a_log.astype(F32)])
    d8 = jnp.concatenate([zeros4, dt_bias.astype(F32)])
    arow = jnp.pad(a8, (0, LANES - 2 * H_A)).reshape(1, LANES)
    drow = jnp.pad(d8, (0, LANES - 2 * H_A)).reshape(1, LANES)
    acol = a8.reshape(2 * H_A, 1)
    dcol = d8.reshape(2 * H_A, 1)
    halves = jnp.stack([peer_k1, peer_k2]).astype(F32)
    eye = jnp.eye(2 * PEER_HEADS, dtype=F32)
    kblk = jnp.einsum("gj,gnd->gnjd", eye, jnp.tile(halves, (PEER_HEADS, 1, 1)))
    kblk = kblk.reshape(2 * PEER_HEADS * N_KEYS, D_MODEL).astype(BF16)
    return (norm1_w.reshape(1, D_MODEL), w_main, w_ba, w_bat, conv_w, arow, drow, acol, dcol,
            gdn_norm_w.reshape(1, DK_A), subln_w.reshape(1, DV_B), w_out.astype(BF16),
            norm2_w.reshape(1, D_MODEL), peer_wq.astype(BF16), kblk, peer_u.astype(BF16),
            peer_v.T.astype(BF16), norm_f_w.reshape(1, D_MODEL))


def kernel(x_prompt, x_sample, cache_k, cache_v, state_conv, state_gdn, norm1_w, w_in, conv_w, a_log, dt_bias, gdn_norm_w, lam_q1, lam_k1, lam_q2, lam_k2, subln_w, w_out, norm2_w, peer_wq, peer_k1, peer_k2, peer_u, peer_v, norm_f_w):
    depth = w_in.shape[0]
    assert depth == 1, "the final norm is fused into the (single) layer's last kernel"
    l = 0
    lam_init = 0.8 - 0.6 * math.exp(-0.3 * l)
    lam = (jnp.exp(jnp.sum(lam_q1[l].astype(F32) * lam_k1[l].astype(F32)))
           - jnp.exp(jnp.sum(lam_q2[l].astype(F32) * lam_k2[l].astype(F32))) + lam_init).reshape(1)
    wts = _prep_weights(norm1_w[l], w_in[l], conv_w[l], a_log[l], dt_bias[l], gdn_norm_w[l], subln_w[l],
                        w_out[l], norm2_w[l], peer_wq[l], peer_k1[l], peer_k2[l], peer_u[l], peer_v[l], norm_f_w)
    bp = x_prompt.shape[0]
    conv0 = jnp.zeros((bp, CONV_W - 1, CONV_CH), F32)
    ssm0 = jnp.zeros((bp, H_A, DK_A, DK_A), F32)
    yp, k1, v1, c1, s1 = _layer(x_prompt, conv0, ssm0, None, None, lam, lam_init, wts)
    ys, k2, v2, c2, s2 = _layer(x_sample, state_conv[l], state_gdn[l], cache_k[l], cache_v[l], lam, lam_init, wts)
    st = lambda a: a[None]
    return (yp, ys, st(k1), st(v1), st(c1), st(s1), st(k2), st(v2), st(c2), st(s2))
```

```python
import functools
import math

import jax
import jax.numpy as jnp
from jax import lax
from jax.experimental import pallas as pl
from jax.experimental.pallas import tpu as pltpu

F32 = jnp.float32
BF16 = jnp.bfloat16
HIGHEST = lax.Precision.HIGHEST

D_MODEL = 1024
CHUNK = 64
H_A = 4
DK_A = 128
CONV_W = 4
W_A = H_A * DK_A
CONV_CH = 3 * W_A
H_B = 4
DH_B = 64
DV_B = 128
W_B = H_B * DV_B
DIFF_EPS = 1e-5
N_KEYS = 128
N_EXPERTS = N_KEYS * N_KEYS
PEER_HEADS = 8
PEER_TOPK = 16
PEER_DK_HALF = 64
PEER_NA = 8
EPS = 1e-6
LANES = 128
SUBLANES = 8
VMEM_LIMIT = 56 * 1024 * 1024

NT_DIMS = (((1,), (1,)), ((), ()))
TN_DIMS = (((0,), (0,)), ((), ()))


def _sigmoid(x):
    return 1.0 / (1.0 + jnp.exp(-x))


def _softplus(x):
    return jnp.maximum(x, 0.0) + jnp.log(1.0 + jnp.exp(-jnp.abs(x)))


def _dot_hi(a, b):
    return jnp.dot(a, b, precision=HIGHEST, preferred_element_type=F32)


def _dot_bf(a, b, dims=(((1,), (0,)), ((), ()))):
    return lax.dot_general(a.astype(BF16), b.astype(BF16), dims, preferred_element_type=F32)


def _split_bf(a):
    hi = a.astype(BF16)
    return hi, (a - hi.astype(F32)).astype(BF16)


def _dot_split(a, b):
    d = lambda x, y: jnp.dot(x, y, preferred_element_type=F32)
    return d(a[0], b[0]) + (d(a[0], b[1]) + d(a[1], b[0]))


def _inproj_kernel(x_ref, n1_ref, w_ref, wba_ref, wbat_ref,
                   conv_ref, z_ref, qb_ref, kb_ref, vb_ref, kbh_ref, vbh_ref, ba_ref, bat_ref):
    x = x_ref[...]
    ms = jnp.mean(x * x, axis=-1, keepdims=True)
    h = (x * lax.rsqrt(ms + EPS) * n1_ref[...]).astype(BF16)

    def mm(c0, c1):
        return jnp.dot(h, w_ref[:, c0:c1], preferred_element_type=F32)

    conv_ref[...] = mm(0, CONV_CH)
    z_ref[...] = mm(CONV_CH, CONV_CH + W_A)
    o = CONV_CH + W_A
    qb_ref[...] = (mm(o, o + W_B) * (DH_B ** -0.5)).astype(BF16)
    kb = mm(o + W_B, o + 2 * W_B)
    kb_ref[...] = kb
    kbh_ref[...] = kb.astype(BF16)
    vb = mm(o + 2 * W_B, o + 3 * W_B)
    vb_ref[...] = vb
    vbh_ref[...] = vb.astype(BF16)
    ba_ref[...] = jnp.dot(h, wba_ref[...], preferred_element_type=F32)
    bat_ref[...] = lax.dot_general(wbat_ref[...], h, NT_DIMS, preferred_element_type=F32)


def _inproj(x2d, n1, w_main, w_ba, w_bat, tm):
    T = x2d.shape[0]
    row = lambda c: pl.BlockSpec((tm, c), lambda i: (i, 0))
    full = lambda a: pl.BlockSpec(a.shape, lambda i: (0, 0))
    out_shape = (
        jax.ShapeDtypeStruct((T, CONV_CH), F32),
        jax.ShapeDtypeStruct((T, W_A), F32),
        jax.ShapeDtypeStruct((T, W_B), BF16),
        jax.ShapeDtypeStruct((T, W_B), F32),
        jax.ShapeDtypeStruct((T, W_B), F32),
        jax.ShapeDtypeStruct((T, W_B), BF16),
        jax.ShapeDtypeStruct((T, W_B), BF16),
        jax.ShapeDtypeStruct((T, LANES), F32),
        jax.ShapeDtypeStruct((SUBLANES, T), F32),
    )
    out_specs = (row(CONV_CH), row(W_A), row(W_B), row(W_B), row(W_B), row(W_B), row(W_B), row(LANES),
                 pl.BlockSpec((SUBLANES, tm), lambda i: (0, i)))
    return pl.pallas_call(
        _inproj_kernel,
        out_shape=out_shape,
        grid=(T // tm,),
        in_specs=[row(D_MODEL), full(n1), full(w_main), full(w_ba), full(w_bat)],
        out_specs=out_specs,
        compiler_params=pltpu.CompilerParams(dimension_semantics=("arbitrary",), vmem_limit_bytes=VMEM_LIMIT),
        name="inproj",
    )(x2d, n1, w_main, w_ba, w_bat)


def _gdn_kernel(conv_ref, z_ref, ba_ref, bat_ref, prev_ref, s0_ref, cw_ref, arow_ref, drow_ref,
                acol_ref, dcol_ref, nw_ref, o_ref, sout_ref, up_ref, s_ref, *, C, nc):
    t = pl.program_id(1)
    tt = nc * C

    @pl.when(t == 0)
    def _():
        up_ref[0:SUBLANES, :] = prev_ref[...]
        s_ref[...] = s0_ref[...]

    up_ref[SUBLANES:SUBLANES + tt, :] = conv_ref[...]
    y = up_ref[pl.ds(SUBLANES - CONV_W + 1, tt), :] * cw_ref[0:1, :]
    for j in range(1, CONV_W):
        y = y + up_ref[pl.ds(SUBLANES - CONV_W + 1 + j, tt), :] * cw_ref[j:j + 1, :]
    qkv = y * _sigmoid(y)
    tail = up_ref[tt:tt + SUBLANES, :]
    up_ref[0:SUBLANES, :] = tail

    ri = lax.broadcasted_iota(jnp.int32, (C, C), 0)
    ci = lax.broadcasted_iota(jnp.int32, (C, C), 1)
    incl = ri >= ci
    strict = ri > ci
    tril = incl.astype(F32)
    triu = (ri <= ci).astype(F32)
    eye = (ri == ci).astype(F32)

    ba = ba_ref[...]
    beta_all = _sigmoid(ba)
    g_all = -jnp.exp(arow_ref[...]) * _softplus(ba + drow_ref[...])
    nw = nw_ref[...]

    probs = []
    for c in range(nc):
        rows = slice(c * C, (c + 1) * C)
        gc_all = _dot_hi(tril, g_all[rows])
        g_t = -jnp.exp(acol_ref[...]) * _softplus(bat_ref[c] + dcol_ref[...])
        gc_t = _dot_hi(g_t, triu)
        for h in range(H_A):
            gcc = gc_all[:, H_A + h:H_A + h + 1]
            gcr = gc_t[H_A + h:H_A + h + 1, :]
            beta = beta_all[rows, h:h + 1]
            decay = jnp.where(incl, jnp.exp(jnp.where(incl, gcc - gcr, 0.0)), 0.0)
            qh = qkv[rows, h * DK_A:(h + 1) * DK_A]
            kh = qkv[rows, W_A + h * DK_A:W_A + (h + 1) * DK_A]
            vh = qkv[rows, 2 * W_A + h * DK_A:2 * W_A + (h + 1) * DK_A]
            qh = qh * lax.rsqrt(jnp.sum(qh * qh, axis=-1, keepdims=True) + EPS) * (DK_A ** -0.5)
            kh = kh * lax.rsqrt(jnp.sum(kh * kh, axis=-1, keepdims=True) + EPS)
            kb = kh * beta
            n = jnp.where(strict, -(_dot_bf(kb, kh, NT_DIMS) * decay), 0.0)
            egc = jnp.exp(gcc)
            g_last = gcc[C - 1:C, :]
            probs.append(dict(h=h, rows=rows, n=n, vb=(vh * beta).astype(BF16),
                              kbe=(kb * egc).astype(BF16), qg=(qh * egc).astype(BF16),
                              attn=(_dot_bf(qh, kh, NT_DIMS) * decay).astype(BF16),
                              kd=(kh * jnp.exp(g_last - gcc)).astype(BF16), eg=jnp.exp(g_last)))

    tinv = [eye + pr["n"] for pr in probs]
    pw = [_split_bf(pr["n"]) for pr in probs]
    for _ in range(int(math.log2(C)) - 1):
        pw = [_split_bf(_dot_split(p, p)) for p in pw]
        tinv = [ti + _dot_split(_split_bf(ti), p) for ti, p in zip(tinv, pw)]
    for pr, ti in zip(probs, tinv):
        ti = ti.astype(BF16)
        pr["u"] = jnp.dot(ti, pr["vb"], preferred_element_type=F32)
        pr["w"] = jnp.dot(ti, pr["kbe"], preferred_element_type=F32).astype(BF16)

    state = [s_ref[h] for h in range(H_A)]
    for pr in probs:
        h, rows = pr["h"], pr["rows"]
        s = state[h]
        s_bf = s.astype(BF16)
        v_new = (pr["u"] - jnp.dot(pr["w"], s_bf, preferred_element_type=F32)).astype(BF16)
        o = (jnp.dot(pr["qg"], s_bf, preferred_element_type=F32)
             + jnp.dot(pr["attn"], v_new, preferred_element_type=F32))
        state[h] = s * pr["eg"] + lax.dot_general(pr["kd"], v_new, TN_DIMS, preferred_element_type=F32)
        on = o * lax.rsqrt(jnp.mean(o * o, axis=-1, keepdims=True) + EPS) * nw
        zz = z_ref[rows, h * DK_A:(h + 1) * DK_A]
        o_ref[rows, h * DK_A:(h + 1) * DK_A] = (on * (zz * _sigmoid(zz))).astype(o_ref.dtype)
    for h in range(H_A):
        s_ref[h] = state[h]

    @pl.when(t == pl.num_programs(1) - 1)
    def _():
        sout_ref[...] = s_ref[...]


def _gdn(conv_in, z, ba, bat, prev8, s0, conv_w, arow, drow, acol, dcol, nw, C, nc):
    B, T, _ = conv_in.shape
    tt = nc * C
    tok = lambda c: pl.BlockSpec((None, tt, c), lambda b, t: (b, t, 0))
    full2 = lambda a: pl.BlockSpec(a.shape, lambda b, t: (0, 0))
    return pl.pallas_call(
        functools.partial(_gdn_kernel, C=C, nc=nc),
        out_shape=(jax.ShapeDtypeStruct((B, T, W_A), BF16),
                   jax.ShapeDtypeStruct((B, H_A, DK_A, DK_A), F32)),
        grid=(B, T // tt),
        in_specs=[tok(CONV_CH), tok(W_A), tok(LANES),
                  pl.BlockSpec((None, nc, SUBLANES, C), lambda b, t: (b, t, 0, 0)),
                  pl.BlockSpec((None, SUBLANES, CONV_CH), lambda b, t: (b, 0, 0)),
                  pl.BlockSpec((None, H_A, DK_A, DK_A), lambda b, t: (b, 0, 0, 0)),
                  full2(conv_w), full2(arow), full2(drow), full2(acol), full2(dcol), full2(nw)],
        out_specs=(tok(W_A), pl.BlockSpec((None, H_A, DK_A, DK_A), lambda b, t: (b, 0, 0, 0))),
        scratch_shapes=[pltpu.VMEM((SUBLANES + tt, CONV_CH), F32), pltpu.VMEM((H_A, DK_A, DK_A), F32)],
        compiler_params=pltpu.CompilerParams(dimension_semantics=("arbitrary", "arbitrary"),
                                             vmem_limit_bytes=VMEM_LIMIT),
        name="gdn",
    )(conv_in, z, ba, bat, prev8, s0, conv_w, arow, drow, acol, dcol, nw)


def _split_maps(q):
    lane = lax.broadcasted_iota(jnp.int32, q.shape, 1)
    zero = jnp.zeros_like(q)
    return jnp.concatenate([jnp.where(lane < DH_B, q, zero), jnp.where(lane >= DH_B, q, zero)], axis=0)


def _subln(acc, l, lam, sw, tq, lam_init):
    o = acc[:tq] / l[:tq] - lam * (acc[tq:] / l[tq:])
    return o * lax.rsqrt(jnp.mean(o * o, axis=-1, keepdims=True) + DIFF_EPS) * sw * (1.0 - lam_init)


def _attn_prompt_kernel(lam_ref, q_ref, k_ref, v_ref, sw_ref, o_ref, qs_ref, s_ref, m_ref, l_ref, acc_ref,
                        *, tq, rc, lam_init):
    qi = pl.program_id(1)
    qs_ref[...] = _split_maps(q_ref[...])
    m_ref[...] = jnp.full(m_ref.shape, -jnp.inf, F32)
    l_ref[...] = jnp.zeros(l_ref.shape, F32)
    acc_ref[...] = jnp.zeros(acc_ref.shape, F32)
    chunks = [slice(c * rc, (c + 1) * rc) for c in range(2 * tq // rc)]
    rep = tq // LANES

    def scores(j):
        kblk = k_ref[pl.ds(pl.multiple_of(j * tq, tq), tq), :]
        for rows in chunks:
            s_ref[rows, :] = lax.dot_general(qs_ref[rows, :], kblk, NT_DIMS, preferred_element_type=F32)

    def softmax_pv(j, masked):
        vblk = v_ref[pl.ds(pl.multiple_of(j * tq, tq), tq), :]
        for rows in chunks:
            s = s_ref[rows, :]
            if masked:
                r = lax.broadcasted_iota(jnp.int32, s.shape, 0) + (rows.start % tq)
                c = lax.broadcasted_iota(jnp.int32, s.shape, 1)
                s = jnp.where((c // CHUNK) <= (r // CHUNK), s, -jnp.inf)
            m_old = m_ref[rows, :]
            m_new = jnp.maximum(m_old, jnp.max(s, axis=-1, keepdims=True))
            alpha = jnp.exp(m_old - m_new)
            p = jnp.exp(s - jnp.concatenate([m_new] * rep, axis=1))
            l_ref[rows, :] = alpha * l_ref[rows, :] + jnp.sum(p, axis=-1, keepdims=True)
            acc_ref[rows, :] = alpha * acc_ref[rows, :] + jnp.dot(p.astype(BF16), vblk,
                                                                  preferred_element_type=F32)
            m_ref[rows, :] = m_new

    scores(qi)
    softmax_pv(qi, True)

    @pl.when(qi > 0)
    def _():
        scores(0)

        def body(j, carry):
            softmax_pv(j, False)
            scores(j + 1)
            return carry

        lax.fori_loop(0, qi - 1, body, 0)
        softmax_pv(qi - 1, False)

    o_ref[...] = _subln(acc_ref[...], l_ref[...], lam_ref[0], sw_ref[...], tq, lam_init).astype(o_ref.dtype)


def _attn_prompt(lam, q, k, v, sw, tq, lam_init):
    T = q.shape[0]
    return pl.pallas_call(
        functools.partial(_attn_prompt_kernel, tq=tq, rc=min(256, tq), lam_init=lam_init),
        out_shape=jax.ShapeDtypeStruct((T, W_B), BF16),
        grid=(H_B, T // tq),
        in_specs=[pl.BlockSpec(memory_space=pltpu.SMEM),
                  pl.BlockSpec((tq, DV_B), lambda h, i: (i, h)),
                  pl.BlockSpec((T, DV_B), lambda h, i: (0, h)),
                  pl.BlockSpec((T, DV_B), lambda h, i: (0, h)),
                  pl.BlockSpec((1, DV_B), lambda h, i: (0, 0))],
        out_specs=pl.BlockSpec((tq, DV_B), lambda h, i: (i, h)),
        scratch_shapes=[pltpu.VMEM((2 * tq, DV_B), BF16), pltpu.VMEM((2 * tq, tq), F32),
                        pltpu.VMEM((2 * tq, LANES), F32), pltpu.VMEM((2 * tq, LANES), F32),
                        pltpu.VMEM((2 * tq, DV_B), F32)],
        compiler_params=pltpu.CompilerParams(dimension_semantics=("arbitrary", "arbitrary"),
                                             vmem_limit_bytes=VMEM_LIMIT),
        name="attn_prompt",
    )(lam, q, k, v, sw)


def _attn_sample_kernel(lam_ref, q_ref, kn_ref, vn_ref, ck_ref, cv_ref, sw_ref, o_ref, *, tq, lam_init):
    qs = _split_maps(q_ref[...])
    kc = ck_ref[...].astype(BF16)
    vc = cv_ref[...].astype(BF16)
    sc = lax.dot_general(qs, kc, NT_DIMS, preferred_element_type=F32)
    sn = lax.dot_general(qs, kn_ref[...], NT_DIMS, preferred_element_type=F32)
    m = jnp.maximum(jnp.max(sc, axis=-1, keepdims=True), jnp.max(sn, axis=-1, keepdims=True))
    pc = jnp.exp(sc - m)
    pn = jnp.exp(sn - m)
    l = jnp.sum(pc, axis=-1, keepdims=True) + jnp.sum(pn, axis=-1, keepdims=True)
    acc = (jnp.dot(pc.astype(BF16), vc, preferred_element_type=F32)
           + jnp.dot(pn.astype(BF16), vn_ref[...], preferred_element_type=F32))
    o_ref[...] = _subln(acc, l, lam_ref[0], sw_ref[...], tq, lam_init).astype(o_ref.dtype)


def _attn_sample(lam, q, kn, vn, ck, cv, sw, lam_init):
    B, tq, _ = q.shape
    P = ck.shape[1]
    new = pl.BlockSpec((None, tq, DV_B), lambda b, h: (b, 0, h))
    cache = pl.BlockSpec((None, P, DV_B), lambda b, h: (b, 0, h))
    return pl.pallas_call(
        functools.partial(_attn_sample_kernel, tq=tq, lam_init=lam_init),
        out_shape=jax.ShapeDtypeStruct((B, tq, W_B), BF16),
        grid=(B, H_B),
        in_specs=[pl.BlockSpec(memory_space=pltpu.SMEM), new, new, new, cache, cache,
                  pl.BlockSpec((1, DV_B), lambda b, h: (0, 0))],
        out_specs=new,
        compiler_params=pltpu.CompilerParams(dimension_semantics=("arbitrary", "arbitrary"),
                                             vmem_limit_bytes=VMEM_LIMIT),
        name="attn_sample",
    )(lam, q, kn, vn, ck, cv, sw)


def _outproj_kernel(x_ref, oa_ref, ob_ref, w_ref, y_ref):
    y_ref[...] = (x_ref[...]
                  + jnp.dot(oa_ref[...], w_ref[0:W_A, :], preferred_element_type=F32)
                  + jnp.dot(ob_ref[...], w_ref[W_A:W_A + W_B, :], preferred_element_type=F32))


def _outproj(x2d, oa, ob, w_out, tm):
    T = x2d.shape[0]
    row = lambda c: pl.BlockSpec((tm, c), lambda i: (i, 0))
    return pl.pallas_call(
        _outproj_kernel,
        out_shape=jax.ShapeDtypeStruct((T, D_MODEL), F32),
        grid=(T // tm,),
        in_specs=[row(D_MODEL), row(W_A), row(W_B), pl.BlockSpec(w_out.shape, lambda i: (0, 0))],
        out_specs=row(D_MODEL),
        compiler_params=pltpu.CompilerParams(dimension_semantics=("arbitrary",), vmem_limit_bytes=VMEM_LIMIT),
        name="outproj",
    )(x2d, oa, ob, w_out)


_NEG_INF = float("-inf")
_PAIR_SLAB = SUBLANES


def _top16_ranked(s, exact):
    n = s.shape[0]
    iota = lax.broadcasted_iota(jnp.int32, s.shape, 0)
    rank = jnp.full(s.shape, float(PEER_TOPK), F32)
    vals = []
    for p in range(PEER_TOPK):
        m = jnp.max(s, axis=0, keepdims=True)
        if exact:
            hit = iota == jnp.min(jnp.where(s == m, iota, n), axis=0, keepdims=True)
        else:
            hit = s == m
        s = jnp.where(hit, _NEG_INF, s)
        rank = jnp.where(hit, float(p), rank)
        vals.append(m)
    picked = jnp.sum(jnp.where(rank < float(PEER_TOPK), 1.0, 0.0), axis=0, keepdims=True)
    return vals, rank, picked


def _pair_counts(v1, v2, exact):
    t = v1[0].shape[1]
    v1m = jnp.concatenate(v1, axis=0)
    v2m = jnp.concatenate(v2, axis=0)
    v2s = v2m[0:_PAIR_SLAB]
    i16 = lax.broadcasted_iota(jnp.int32, (PEER_TOPK, t), 0)
    i8 = lax.broadcasted_iota(jnp.int32, (_PAIR_SLAB, t), 0)
    slabs = [v1[0] + v2m, v1[1] + v2s]
    flats = [i16, PEER_TOPK + i8]
    for p in range(2, _PAIR_SLAB):
        slabs.append(jnp.where(i8 < PEER_TOPK // (p + 1), v1[p] + v2s, _NEG_INF))
        flats.append(PEER_TOPK * p + i8)
    slabs.append(v1m[_PAIR_SLAB:] + v2[0])
    flats.append(PEER_TOPK * (_PAIR_SLAB + i8))
    cand = jnp.concatenate(slabs, axis=0)
    flat = jnp.concatenate(flats, axis=0)
    big = PEER_TOPK * PEER_TOPK
    c = cand
    for _ in range(PEER_TOPK):
        m = jnp.max(c, axis=0, keepdims=True)
        if exact:
            hit = flat == jnp.min(jnp.where(c == m, flat, big), axis=0, keepdims=True)
        else:
            hit = c == m
        c = jnp.where(hit, _NEG_INF, c)
    sel = c != cand
    self = jnp.where(sel, 1.0, 0.0)
    picked = jnp.sum(self, axis=0, keepdims=True)
    counts = [jnp.sum(self[0:PEER_TOPK], axis=0, keepdims=True)]
    for p in range(1, _PAIR_SLAB):
        r0 = PEER_TOPK + (p - 1) * _PAIR_SLAB
        counts.append(jnp.sum(self[r0:r0 + _PAIR_SLAB], axis=0, keepdims=True))
    r0 = PEER_TOPK + (_PAIR_SLAB - 1) * _PAIR_SLAB
    for r in range(_PAIR_SLAB):
        counts.append(self[r0 + r:r0 + r + 1])
    zsum = jnp.sum(jnp.where(sel, jnp.exp(cand - cand[0:1]), 0.0), axis=0, keepdims=True)
    return counts, zsum, picked


def _peer_score_kernel(x_ref, n2_ref, wq_ref, kblk_ref, ht_ref, c1_ref, e1_ref, r2_ref, e2_ref, st_ref, *, tt):
    x = x_ref[...]
    ms = jnp.mean(x * x, axis=-1, keepdims=True)
    h = x * lax.rsqrt(ms + EPS) * n2_ref[...]
    ht_ref[...] = h.T.astype(BF16)
    q = jnp.dot(h.astype(BF16), wq_ref[...], preferred_element_type=F32)
    st_ref[...] = lax.dot_general(kblk_ref[...], q.astype(BF16), NT_DIMS, preferred_element_type=F32)

    def select(exact):
        def head(hd, tied):
            r1 = pl.multiple_of(hd * (2 * N_KEYS), 2 * N_KEYS)
            for g in range(tt // LANES):
                cols = slice(g * LANES, (g + 1) * LANES)
                s1 = st_ref[pl.ds(r1, N_KEYS), cols]
                s2 = st_ref[pl.ds(r1 + N_KEYS, N_KEYS), cols]
                v1, rank1, n1 = _top16_ranked(s1, exact)
                v2, rank2, n2 = _top16_ranked(s2, exact)
                counts, zsum, n12 = _pair_counts(v1, v2, exact)
                k = float(PEER_TOPK)
                tied = jnp.maximum(tied, jnp.where((n1 != k) | (n2 != k) | (n12 != k), 1.0, 0.0))
                c1 = jnp.zeros(s1.shape, F32)
                for p in range(PEER_TOPK):
                    c1 = jnp.where(rank1 == float(p), counts[p], c1)
                c1_ref[hd, :, cols] = c1
                e1_ref[hd, :, cols] = jnp.exp(s1 - v1[0])
                r2_ref[hd, :, cols] = rank2.astype(r2_ref.dtype)
                e2_ref[hd, :, cols] = (jnp.exp(s2 - v2[0]) * (1.0 / zsum)).astype(e2_ref.dtype)
            return tied
        return head

    tied = lax.fori_loop(0, PEER_HEADS, select(False), jnp.zeros((1, LANES), F32))

    @pl.when(jnp.max(tied) > 0.0)
    def _():
        lax.fori_loop(0, PEER_HEADS, select(True), jnp.zeros((1, LANES), F32))


def _peer_score(x1, n2, wq, kblk, tt):
    T = x1.shape[0]
    spec = pl.BlockSpec((PEER_HEADS, N_KEYS, tt), lambda i: (0, 0, i))
    f32s = jax.ShapeDtypeStruct((PEER_HEADS, N_KEYS, T), F32)
    bf16s = jax.ShapeDtypeStruct((PEER_HEADS, N_KEYS, T), BF16)
    return pl.pallas_call(
        functools.partial(_peer_score_kernel, tt=tt),
        out_shape=(jax.ShapeDtypeStruct((D_MODEL, T), BF16), f32s, f32s, bf16s, bf16s),
        grid=(T // tt,),
        in_specs=[pl.BlockSpec((tt, D_MODEL), lambda i: (i, 0)),
                  pl.BlockSpec(n2.shape, lambda i: (0, 0)),
                  pl.BlockSpec(wq.shape, lambda i: (0, 0)),
                  pl.BlockSpec(kblk.shape, lambda i: (0, 0))],
        out_specs=(pl.BlockSpec((D_MODEL, tt), lambda i: (0, i)), spec, spec, spec, spec),
        scratch_shapes=[pltpu.VMEM((2 * PEER_HEADS * N_KEYS, tt), F32)],
        compiler_params=pltpu.CompilerParams(dimension_semantics=("arbitrary",), vmem_limit_bytes=VMEM_LIMIT),
        name="peer_score",
    )(x1, n2, wq, kblk)


def _peer_expert_kernel(ht_ref, u_ref, vt_ref, c1_ref, e1_ref, r2_ref, e2_ref, x_ref, nf_ref,
                        y_ref, acc_ref, ga_ref, hh_ref, r2b_ref, e2b_ref, *, na, tt):
    e = pl.program_id(1)

    @pl.when(e == 0)
    def _():
        acc_ref[...] = jnp.zeros(acc_ref.shape, F32)
        r2b_ref[...] = r2_ref[...]
        e2b_ref[...] = e2_ref[...]

    hh_ref[...] = jnp.dot(u_ref[...], ht_ref[...], preferred_element_type=F32)
    pack = 2 * SUBLANES
    zero = jnp.zeros((N_KEYS, LANES), BF16)
    for a in range(na):
        rows = slice(a * N_KEYS, (a + 1) * N_KEYS)
        for ts in range(tt // LANES):
            cols = slice(ts * LANES, (ts + 1) * LANES)
            g = zero
            for hd in range(PEER_HEADS):
                c1row = jnp.broadcast_to(c1_ref[hd, a:a + 1, cols], (pack, LANES)).astype(BF16)
                e1row = jnp.broadcast_to(e1_ref[hd, a:a + 1, cols], (pack, LANES)).astype(BF16)
                c1b = jnp.concatenate([c1row] * (N_KEYS // pack), axis=0)
                e1b = jnp.concatenate([e1row] * (N_KEYS // pack), axis=0)
                sel = r2b_ref[hd, :, cols] < c1b
                g = g + jnp.where(sel, e1b * e2b_ref[hd, :, cols], zero)
            x = hh_ref[rows, cols]
            act = 0.5 * x * (1.0 + lax.erf(x * (2.0 ** -0.5)))
            ga_ref[rows, cols] = g * act.astype(BF16)
    acc_ref[...] += jnp.dot(vt_ref[...], ga_ref[...], preferred_element_type=F32)

    @pl.when(e == pl.num_programs(1) - 1)
    def _():
        xo = x_ref[...] + acc_ref[...].T
        ms = jnp.mean(xo * xo, axis=-1, keepdims=True)
        y_ref[...] = xo * lax.rsqrt(ms + EPS) * nf_ref[...]


def _peer_expert(ht, u, vt, c1, e1, r2, e2, x1, nf, tt, na):
    T = x1.shape[0]
    et = na * N_KEYS
    tok3 = pl.BlockSpec((PEER_HEADS, N_KEYS, tt), lambda i, e: (0, 0, i))
    exp3 = pl.BlockSpec((PEER_HEADS, na, tt), lambda i, e: (0, e, i))
    return pl.pallas_call(
        functools.partial(_peer_expert_kernel, na=na, tt=tt),
        out_shape=jax.ShapeDtypeStruct((T, D_MODEL), F32),
        grid=(T // tt, N_EXPERTS // et),
        in_specs=[pl.BlockSpec((D_MODEL, tt), lambda i, e: (0, i)),
                  pl.BlockSpec((et, D_MODEL), lambda i, e: (e, 0)),
                  pl.BlockSpec((None, D_MODEL, et), lambda i, e: (e, 0, 0)),
                  exp3, exp3, tok3, tok3,
                  pl.BlockSpec((tt, D_MODEL), lambda i, e: (i, 0)),
                  pl.BlockSpec(nf.shape, lambda i, e: (0, 0))],
        out_specs=pl.BlockSpec((tt, D_MODEL), lambda i, e: (i, 0)),
        scratch_shapes=[pltpu.VMEM((D_MODEL, tt), F32), pltpu.VMEM((et, tt), BF16), pltpu.VMEM((et, tt), F32),
                        pltpu.VMEM((PEER_HEADS, N_KEYS, tt), BF16), pltpu.VMEM((PEER_HEADS, N_KEYS, tt), BF16)],
        compiler_params=pltpu.CompilerParams(dimension_semantics=("arbitrary", "arbitrary"),
                                             vmem_limit_bytes=VMEM_LIMIT),
        name="peer_expert",
    )(ht, u, vt, c1, e1, r2, e2, x1, nf)


def _pick_tile(T, prefs):
    for t in prefs:
        if T % t == 0:
            return t
    raise ValueError(f"no tile for {T}")


def _layer(x, conv_prev, ssm_prev, k_cache, v_cache, lam, lam_init, wts):
    (n1, w_main, w_ba, w_bat, conv_w, arow, drow, acol, dcol, gnw, sw, w_out, n2, wq, kblk, u, vt, nf) = wts
    B, T, _ = x.shape
    n_tok = B * T
    x2d = x.reshape(n_tok, D_MODEL)
    tm = _pick_tile(n_tok, (256, 128))
    conv_in, z, qb, kb, vb, kbh, vbh, ba, bat = _inproj(x2d, n1, w_main, w_ba, w_bat, tm)

    C = min(CHUNK, T)
    prev8 = jnp.pad(conv_prev.astype(F32), ((0, 0), (SUBLANES - CONV_W + 1, 0), (0, 0)))
    bat4 = bat.reshape(SUBLANES, B, T // C, C).transpose(1, 2, 0, 3)
    o_a, ssm_new = _gdn(conv_in.reshape(B, T, CONV_CH), z.reshape(B, T, W_A), ba.reshape(B, T, LANES), bat4,
                        prev8, ssm_prev.astype(F32), conv_w, arow, drow, acol, dcol, gnw, C,
                        2 if (T // C) % 2 == 0 else 1)
    conv_new = conv_in.reshape(B, T, CONV_CH)[:, T - (CONV_W - 1):, :]

    if k_cache is None:
        tq = _pick_tile(T, (512, 256, 128))
        o_b = _attn_prompt(lam, qb, kbh, vbh, sw, tq, lam_init)
    else:
        P = k_cache.shape[1]
        o_b = _attn_sample(lam, qb.reshape(B, T, W_B), kbh.reshape(B, T, W_B), vbh.reshape(B, T, W_B),
                           k_cache.reshape(B, P, W_B), v_cache.reshape(B, P, W_B), sw, lam_init)
        o_b = o_b.reshape(n_tok, W_B)

    x1 = _outproj(x2d, o_a.reshape(n_tok, W_A), o_b, w_out, tm)

    tt = _pick_tile(n_tok, (256, 128))
    ht, c1, e1, r2, e2 = _peer_score(x1, n2, wq, kblk, tt)
    tt2 = _pick_tile(n_tok, (1024, 512, 256, 128))
    y = _peer_expert(ht, u, vt, c1, e1, r2, e2, x1, nf, tt2, PEER_NA)
    return (y.reshape(B, T, D_MODEL), kb.reshape(B, T, H_B, 2 * DH_B), vb.reshape(B, T, H_B, DV_B),
            conv_new, ssm_new)


def _prep_weights(norm1_w, w_in, conv_w, a_log, dt_bias, gdn_norm_w, subln_w, w_out,
                  norm2_w, peer_wq, peer_k1, peer_k2, peer_u, peer_v, norm_f_w):
    o_beta = CONV_CH + W_A
    w_main = jnp.concatenate([w_in[:, :o_beta], w_in[:, o_beta + 2 * H_A:]], axis=1).astype(BF16)
    w_gate = w_in[:, o_beta:o_beta + 2 * H_A]
    w_ba = jnp.pad(w_gate, ((0, 0), (0, LANES - 2 * H_A))).astype(BF16)
    w_bat = w_gate.T.astype(BF16)
    zeros4 = jnp.zeros((H_A,), F32)
    a8 = jnp.concatenate([zeros4, a_log.astype(F32)])
    d8 = jnp.concatenate([zeros4, dt_bias.astype(F32)])
    arow = jnp.pad(a8, (0, LANES - 2 * H_A)).reshape(1, LANES)
    drow = jnp.pad(d8, (0, LANES - 2 * H_A)).reshape(1, LANES)
    acol = a8.reshape(2 * H_A, 1)
    dcol = d8.reshape(2 * H_A, 1)
    halves = jnp.stack([peer_k1, peer_k2]).astype(F32)
    eye = jnp.eye(2 * PEER_HEADS, dtype=F32)
    kblk = jnp.einsum("gj,gnd->gnjd", eye, jnp.tile(halves, (PEER_HEADS, 1, 1)))
    kblk = kblk.reshape(2 * PEER_HEADS * N_KEYS, D_MODEL).astype(BF16)
    et = PEER_NA * N_KEYS
    return (norm1_w.reshape(1, D_MODEL), w_main, w_ba, w_bat, conv_w, arow, drow, acol, dcol,
            gdn_norm_w.reshape(1, DK_A), subln_w.reshape(1, DV_B), w_out.astype(BF16),
            norm2_w.reshape(1, D_MODEL), peer_wq.astype(BF16), kblk, peer_u.astype(BF16),
            peer_v.reshape(N_EXPERTS // et, et, D_MODEL).transpose(0, 2, 1).astype(BF16),
            norm_f_w.reshape(1, D_MODEL))


def kernel(x_prompt, x_sample, cache_k, cache_v, state_conv, state_gdn, norm1_w, w_in, conv_w, a_log, dt_bias, gdn_norm_w, lam_q1, lam_k1, lam_q2, lam_k2, subln_w, w_out, norm2_w, peer_wq, peer_k1, peer_k2, peer_u, peer_v, norm_f_w):
    depth = w_in.shape[0]
    assert depth == 1, "the final norm is fused into the (single) layer's last kernel"
    l = 0
    lam_init = 0.8 - 0.6 * math.exp(-0.3 * l)
    lam = (jnp.exp(jnp.sum(lam_q1[l].astype(F32) * lam_k1[l].astype(F32)))
           - jnp.exp(jnp.sum(lam_q2[l].astype(F32) * lam_k2[l].astype(F32))) + lam_init).reshape(1)
    wts = _prep_weights(norm1_w[l], w_in[l], conv_w[l], a_log[l], dt_bias[l], gdn_norm_w[l], subln_w[l],
                        w_out[l], norm2_w[l], peer_wq[l], peer_k1[l], peer_k2[l], peer_u[l], peer_v[l], norm_f_w)
    bp = x_prompt.shape[0]
    conv0 = jnp.zeros((bp, CONV_W - 1, CONV_CH), F32)
    ssm0 = jnp.zeros((bp, H_A, DK_A, DK_A), F32)
    yp, k1, v1, c1, s1 = _layer(x_prompt, conv0, ssm0, None, None, lam, lam_init, wts)
    ys, k2, v2, c2, s2 = _layer(x_sample, state_conv[l], state_gdn[l], cache_k[l], cache_v[l], lam, lam_init, wts)
    st = lambda a: a[None]
    return (yp, ys, st(k1), st(v1), st(c1), st(s1), st(k2), st(v2), st(c2), st(s2))
```

```python
import functools
import math

import jax
import jax.numpy as jnp
from jax import lax
from jax.experimental import pallas as pl
from jax.experimental.pallas import tpu as pltpu

F32 = jnp.float32
BF16 = jnp.bfloat16
HIGHEST = lax.Precision.HIGHEST

D_MODEL = 1024
CHUNK = 64
H_A = 4
DK_A = 128
CONV_W = 4
W_A = H_A * DK_A
CONV_CH = 3 * W_A
H_B = 4
DH_B = 64
DV_B = 128
W_B = H_B * DV_B
DIFF_EPS = 1e-5
N_KEYS = 128
N_EXPERTS = N_KEYS * N_KEYS
PEER_HEADS = 8
PEER_TOPK = 16
PEER_DK_HALF = 64
PEER_NA = 8
EPS = 1e-6
LANES = 128
SUBLANES = 8
VMEM_LIMIT = 56 * 1024 * 1024

NT_DIMS = (((1,), (1,)), ((), ()))
TN_DIMS = (((0,), (0,)), ((), ()))


def _sigmoid(x):
    return 1.0 / (1.0 + jnp.exp(-x))


def _softplus(x):
    return jnp.maximum(x, 0.0) + jnp.log(1.0 + jnp.exp(-jnp.abs(x)))


def _dot_hi(a, b):
    return jnp.dot(a, b, precision=HIGHEST, preferred_element_type=F32)


def _dot_bf(a, b, dims=(((1,), (0,)), ((), ()))):
    return lax.dot_general(a.astype(BF16), b.astype(BF16), dims, preferred_element_type=F32)


def _split_bf(a):
    hi = a.astype(BF16)
    return hi, (a - hi.astype(F32)).astype(BF16)


def _dot_split(a, b):
    d = lambda x, y: jnp.dot(x, y, preferred_element_type=F32)
    return d(a[0], b[0]) + (d(a[0], b[1]) + d(a[1], b[0]))


def _inproj_kernel(x_ref, n1_ref, w_ref, wba_ref, wbat_ref,
                   conv_ref, z_ref, qb_ref, kb_ref, vb_ref, kbh_ref, vbh_ref, ba_ref, bat_ref):
    x = x_ref[...]
    ms = jnp.mean(x * x, axis=-1, keepdims=True)
    h = (x * lax.rsqrt(ms + EPS) * n1_ref[...]).astype(BF16)

    def mm(c0, c1):
        return jnp.dot(h, w_ref[:, c0:c1], preferred_element_type=F32)

    conv_ref[...] = mm(0, CONV_CH)
    z_ref[...] = mm(CONV_CH, CONV_CH + W_A)
    o = CONV_CH + W_A
    qb_ref[...] = (mm(o, o + W_B) * (DH_B ** -0.5)).astype(BF16)
    kb = mm(o + W_B, o + 2 * W_B)
    kb_ref[...] = kb
    kbh_ref[...] = kb.astype(BF16)
    vb = mm(o + 2 * W_B, o + 3 * W_B)
    vb_ref[...] = vb
    vbh_ref[...] = vb.astype(BF16)
    ba_ref[...] = jnp.dot(h, wba_ref[...], preferred_element_type=F32)
    bat_ref[...] = lax.dot_general(wbat_ref[...], h, NT_DIMS, preferred_element_type=F32)


def _inproj(x2d, n1, w_main, w_ba, w_bat, tm):
    T = x2d.shape[0]
    row = lambda c: pl.BlockSpec((tm, c), lambda i: (i, 0))
    full = lambda a: pl.BlockSpec(a.shape, lambda i: (0, 0))
    out_shape = (
        jax.ShapeDtypeStruct((T, CONV_CH), F32),
        jax.ShapeDtypeStruct((T, W_A), F32),
        jax.ShapeDtypeStruct((T, W_B), BF16),
        jax.ShapeDtypeStruct((T, W_B), F32),
        jax.ShapeDtypeStruct((T, W_B), F32),
        jax.ShapeDtypeStruct((T, W_B), BF16),
        jax.ShapeDtypeStruct((T, W_B), BF16),
        jax.ShapeDtypeStruct((T, LANES), F32),
        jax.ShapeDtypeStruct((SUBLANES, T), F32),
    )
    out_specs = (row(CONV_CH), row(W_A), row(W_B), row(W_B), row(W_B), row(W_B), row(W_B), row(LANES),
                 pl.BlockSpec((SUBLANES, tm), lambda i: (0, i)))
    return pl.pallas_call(
        _inproj_kernel,
        out_shape=out_shape,
        grid=(T // tm,),
        in_specs=[row(D_MODEL), full(n1), full(w_main), full(w_ba), full(w_bat)],
        out_specs=out_specs,
        compiler_params=pltpu.CompilerParams(dimension_semantics=("arbitrary",), vmem_limit_bytes=VMEM_LIMIT),
        name="inproj",
    )(x2d, n1, w_main, w_ba, w_bat)


def _gdn_kernel(conv_ref, z_ref, ba_ref, bat_ref, prev_ref, s0_ref, cw_ref, arow_ref, drow_ref,
                acol_ref, dcol_ref, nw_ref, o_ref, sout_ref, up_ref, s_ref, *, C, nc):
    t = pl.program_id(1)
    tt = nc * C

    @pl.when(t == 0)
    def _():
        up_ref[0:SUBLANES, :] = prev_ref[...]
        s_ref[...] = s0_ref[...]

    up_ref[SUBLANES:SUBLANES + tt, :] = conv_ref[...]
    y = up_ref[pl.ds(SUBLANES - CONV_W + 1, tt), :] * cw_ref[0:1, :]
    for j in range(1, CONV_W):
        y = y + up_ref[pl.ds(SUBLANES - CONV_W + 1 + j, tt), :] * cw_ref[j:j + 1, :]
    qkv = y * _sigmoid(y)
    tail = up_ref[tt:tt + SUBLANES, :]
    up_ref[0:SUBLANES, :] = tail

    ri = lax.broadcasted_iota(jnp.int32, (C, C), 0)
    ci = lax.broadcasted_iota(jnp.int32, (C, C), 1)
    incl = ri >= ci
    strict = ri > ci
    tril = incl.astype(F32)
    triu = (ri <= ci).astype(F32)
    eye = (ri == ci).astype(F32)

    ba = ba_ref[...]
    beta_all = _sigmoid(ba)
    g_all = -jnp.exp(arow_ref[...]) * _softplus(ba + drow_ref[...])
    nw = nw_ref[...]

    probs = []
    for c in range(nc):
        rows = slice(c * C, (c + 1) * C)
        gc_all = _dot_hi(tril, g_all[rows])
        g_t = -jnp.exp(acol_ref[...]) * _softplus(bat_ref[c] + dcol_ref[...])
        gc_t = _dot_hi(g_t, triu)
        for h in range(H_A):
            gcc = gc_all[:, H_A + h:H_A + h + 1]
            gcr = gc_t[H_A + h:H_A + h + 1, :]
            beta = beta_all[rows, h:h + 1]
            decay = jnp.where(incl, jnp.exp(jnp.where(incl, gcc - gcr, 0.0)), 0.0)
            qh = qkv[rows, h * DK_A:(h + 1) * DK_A]
            kh = qkv[rows, W_A + h * DK_A:W_A + (h + 1) * DK_A]
            vh = qkv[rows, 2 * W_A + h * DK_A:2 * W_A + (h + 1) * DK_A]
            qh = qh * lax.rsqrt(jnp.sum(qh * qh, axis=-1, keepdims=True) + EPS) * (DK_A ** -0.5)
            kh = kh * lax.rsqrt(jnp.sum(kh * kh, axis=-1, keepdims=True) + EPS)
            kb = kh * beta
            n = jnp.where(strict, -(_dot_bf(kb, kh, NT_DIMS) * decay), 0.0)
            egc = jnp.exp(gcc)
            g_last = gcc[C - 1:C, :]
            probs.append(dict(h=h, rows=rows, n=n, vb=(vh * beta).astype(BF16),
                              kbe=(kb * egc).astype(BF16), qg=(qh * egc).astype(BF16),
                              attn=(_dot_bf(qh, kh, NT_DIMS) * decay).astype(BF16),
                              kd=(kh * jnp.exp(g_last - gcc)).astype(BF16), eg=jnp.exp(g_last)))

    tinv = [eye + pr["n"] for pr in probs]
    pw = [_split_bf(pr["n"]) for pr in probs]
    for _ in range(int(math.log2(C)) - 1):
        pw = [_split_bf(_dot_split(p, p)) for p in pw]
        tinv = [ti + _dot_split(_split_bf(ti), p) for ti, p in zip(tinv, pw)]
    for pr, ti in zip(probs, tinv):
        ti = ti.astype(BF16)
        pr["u"] = jnp.dot(ti, pr["vb"], preferred_element_type=F32)
        pr["w"] = jnp.dot(ti, pr["kbe"], preferred_element_type=F32).astype(BF16)

    state = [s_ref[h] for h in range(H_A)]
    for pr in probs:
        h, rows = pr["h"], pr["rows"]
        s = state[h]
        s_bf = s.astype(BF16)
        v_new = (pr["u"] - jnp.dot(pr["w"], s_bf, preferred_element_type=F32)).astype(BF16)
        o = (jnp.dot(pr["qg"], s_bf, preferred_element_type=F32)
             + jnp.dot(pr["attn"], v_new, preferred_element_type=F32))
        state[h] = s * pr["eg"] + lax.dot_general(pr["kd"], v_new, TN_DIMS, preferred_element_type=F32)
        on = o * lax.rsqrt(jnp.mean(o * o, axis=-1, keepdims=True) + EPS) * nw
        zz = z_ref[rows, h * DK_A:(h + 1) * DK_A]
        o_ref[rows, h * DK_A:(h + 1) * DK_A] = (on * (zz * _sigmoid(zz))).astype(o_ref.dtype)
    for h in range(H_A):
        s_ref[h] = state[h]

    @pl.when(t == pl.num_programs(1) - 1)
    def _():
        sout_ref[...] = s_ref[...]


def _gdn(conv_in, z, ba, bat, prev8, s0, conv_w, arow, drow, acol, dcol, nw, C, nc):
    B, T, _ = conv_in.shape
    tt = nc * C
    tok = lambda c: pl.BlockSpec((None, tt, c), lambda b, t: (b, t, 0))
    full2 = lambda a: pl.BlockSpec(a.shape, lambda b, t: (0, 0))
    return pl.pallas_call(
        functools.partial(_gdn_kernel, C=C, nc=nc),
        out_shape=(jax.ShapeDtypeStruct((B, T, W_A), BF16),
                   jax.ShapeDtypeStruct((B, H_A, DK_A, DK_A), F32)),
        grid=(B, T // tt),
        in_specs=[tok(CONV_CH), tok(W_A), tok(LANES),
                  pl.BlockSpec((None, nc, SUBLANES, C), lambda b, t: (b, t, 0, 0)),
                  pl.BlockSpec((None, SUBLANES, CONV_CH), lambda b, t: (b, 0, 0)),
                  pl.BlockSpec((None, H_A, DK_A, DK_A), lambda b, t: (b, 0, 0, 0)),
                  full2(conv_w), full2(arow), full2(drow), full2(acol), full2(dcol), full2(nw)],
        out_specs=(tok(W_A), pl.BlockSpec((None, H_A, DK_A, DK_A), lambda b, t: (b, 0, 0, 0))),
        scratch_shapes=[pltpu.VMEM((SUBLANES + tt, CONV_CH), F32), pltpu.VMEM((H_A, DK_A, DK_A), F32)],
        compiler_params=pltpu.CompilerParams(dimension_semantics=("arbitrary", "arbitrary"),
                                             vmem_limit_bytes=VMEM_LIMIT),
        name="gdn",
    )(conv_in, z, ba, bat, prev8, s0, conv_w, arow, drow, acol, dcol, nw)


def _split_maps(q):
    lane = lax.broadcasted_iota(jnp.int32, q.shape, 1)
    zero = jnp.zeros_like(q)
    return jnp.concatenate([jnp.where(lane < DH_B, q, zero), jnp.where(lane >= DH_B, q, zero)], axis=0)


def _subln(acc, l, lam, sw, tq, lam_init):
    o = acc[:tq] / l[:tq] - lam * (acc[tq:] / l[tq:])
    return o * lax.rsqrt(jnp.mean(o * o, axis=-1, keepdims=True) + DIFF_EPS) * sw * (1.0 - lam_init)


def _attn_prompt_kernel(lam_ref, q_ref, k_ref, v_ref, sw_ref, o_ref, qs_ref, s_ref, m_ref, l_ref, acc_ref,
                        *, tq, rc, lam_init):
    qi = pl.program_id(1)
    qs_ref[...] = _split_maps(q_ref[...])
    m_ref[...] = jnp.full(m_ref.shape, -jnp.inf, F32)
    l_ref[...] = jnp.zeros(l_ref.shape, F32)
    acc_ref[...] = jnp.zeros(acc_ref.shape, F32)
    chunks = [slice(c * rc, (c + 1) * rc) for c in range(2 * tq // rc)]
    rep = tq // LANES

    def scores(j):
        kblk = k_ref[pl.ds(pl.multiple_of(j * tq, tq), tq), :]
        for rows in chunks:
            s_ref[rows, :] = lax.dot_general(qs_ref[rows, :], kblk, NT_DIMS, preferred_element_type=F32)

    def softmax_pv(j, masked):
        vblk = v_ref[pl.ds(pl.multiple_of(j * tq, tq), tq), :]
        for rows in chunks:
            s = s_ref[rows, :]
            if masked:
                r = lax.broadcasted_iota(jnp.int32, s.shape, 0) + (rows.start % tq)
                c = lax.broadcasted_iota(jnp.int32, s.shape, 1)
                s = jnp.where((c // CHUNK) <= (r // CHUNK), s, -jnp.inf)
            m_old = m_ref[rows, :]
            m_new = jnp.maximum(m_old, jnp.max(s, axis=-1, keepdims=True))
            alpha = jnp.exp(m_old - m_new)
            p = jnp.exp(s - jnp.concatenate([m_new] * rep, axis=1))
            l_ref[rows, :] = alpha * l_ref[rows, :] + jnp.sum(p, axis=-1, keepdims=True)
            acc_ref[rows, :] = alpha * acc_ref[rows, :] + jnp.dot(p.astype(BF16), vblk,
                                                                  preferred_element_type=F32)
            m_ref[rows, :] = m_new

    scores(qi)
    softmax_pv(qi, True)

    @pl.when(qi > 0)
    def _():
        scores(0)

        def body(j, carry):
            softmax_pv(j, False)
            scores(j + 1)
            return carry

        lax.fori_loop(0, qi - 1, body, 0)
        softmax_pv(qi - 1, False)

    o_ref[...] = _subln(acc_ref[...], l_ref[...], lam_ref[0], sw_ref[...], tq, lam_init).astype(o_ref.dtype)


def _attn_prompt(lam, q, k, v, sw, tq, lam_init):
    T = q.shape[0]
    return pl.pallas_call(
        functools.partial(_attn_prompt_kernel, tq=tq, rc=min(256, tq), lam_init=lam_init),
        out_shape=jax.ShapeDtypeStruct((T, W_B), BF16),
        grid=(H_B, T // tq),
        in_specs=[pl.BlockSpec(memory_space=pltpu.SMEM),
                  pl.BlockSpec((tq, DV_B), lambda h, i: (i, h)),
                  pl.BlockSpec((T, DV_B), lambda h, i: (0, h)),
                  pl.BlockSpec((T, DV_B), lambda h, i: (0, h)),
                  pl.BlockSpec((1, DV_B), lambda h, i: (0, 0))],
        out_specs=pl.BlockSpec((tq, DV_B), lambda h, i: (i, h)),
        scratch_shapes=[pltpu.VMEM((2 * tq, DV_B), BF16), pltpu.VMEM((2 * tq, tq), F32),
                        pltpu.VMEM((2 * tq, LANES), F32), pltpu.VMEM((2 * tq, LANES), F32),
                        pltpu.VMEM((2 * tq, DV_B), F32)],
        compiler_params=pltpu.CompilerParams(dimension_semantics=("arbitrary", "arbitrary"),
                                             vmem_limit_bytes=VMEM_LIMIT),
        name="attn_prompt",
    )(lam, q, k, v, sw)


def _attn_sample_kernel(lam_ref, q_ref, kn_ref, vn_ref, ck_ref, cv_ref, sw_ref, o_ref, *, tq, lam_init):
    qs = _split_maps(q_ref[...])
    kc = ck_ref[...].astype(BF16)
    vc = cv_ref[...].astype(BF16)
    sc = lax.dot_general(qs, kc, NT_DIMS, preferred_element_type=F32)
    sn = lax.dot_general(qs, kn_ref[...], NT_DIMS, preferred_element_type=F32)
    m = jnp.maximum(jnp.max(sc, axis=-1, keepdims=True), jnp.max(sn, axis=-1, keepdims=True))
    pc = jnp.exp(sc - m)
    pn = jnp.exp(sn - m)
    l = jnp.sum(pc, axis=-1, keepdims=True) + jnp.sum(pn, axis=-1, keepdims=True)
    acc = (jnp.dot(pc.astype(BF16), vc, preferred_element_type=F32)
           + jnp.dot(pn.astype(BF16), vn_ref[...], preferred_element_type=F32))
    o_ref[...] = _subln(acc, l, lam_ref[0], sw_ref[...], tq, lam_init).astype(o_ref.dtype)


def _attn_sample(lam, q, kn, vn, ck, cv, sw, lam_init):
    B, tq, _ = q.shape
    P = ck.shape[1]
    new = pl.BlockSpec((None, tq, DV_B), lambda b, h: (b, 0, h))
    cache = pl.BlockSpec((None, P, DV_B), lambda b, h: (b, 0, h))
    return pl.pallas_call(
        functools.partial(_attn_sample_kernel, tq=tq, lam_init=lam_init),
        out_shape=jax.ShapeDtypeStruct((B, tq, W_B), BF16),
        grid=(B, H_B),
        in_specs=[pl.BlockSpec(memory_space=pltpu.SMEM), new, new, new, cache, cache,
                  pl.BlockSpec((1, DV_B), lambda b, h: (0, 0))],
        out_specs=new,
        compiler_params=pltpu.CompilerParams(dimension_semantics=("arbitrary", "arbitrary"),
                                             vmem_limit_bytes=VMEM_LIMIT),
        name="attn_sample",
    )(lam, q, kn, vn, ck, cv, sw)


def _outproj_kernel(x_ref, oa_ref, ob_ref, w_ref, y_ref):
    y_ref[...] = (x_ref[...]
                  + jnp.dot(oa_ref[...], w_ref[0:W_A, :], preferred_element_type=F32)
                  + jnp.dot(ob_ref[...], w_ref[W_A:W_A + W_B, :], preferred_element_type=F32))


def _outproj(x2d, oa, ob, w_out, tm):
    T = x2d.shape[0]
    row = lambda c: pl.BlockSpec((tm, c), lambda i: (i, 0))
    return pl.pallas_call(
        _outproj_kernel,
        out_shape=jax.ShapeDtypeStruct((T, D_MODEL), F32),
        grid=(T // tm,),
        in_specs=[row(D_MODEL), row(W_A), row(W_B), pl.BlockSpec(w_out.shape, lambda i: (0, 0))],
        out_specs=row(D_MODEL),
        compiler_params=pltpu.CompilerParams(dimension_semantics=("arbitrary",), vmem_limit_bytes=VMEM_LIMIT),
        name="outproj",
    )(x2d, oa, ob, w_out)


_NEG_INF = float("-inf")
_PAIR_SLAB = SUBLANES
PEER_SLOTS = PEER_HEADS * PEER_TOPK


def _top16(s):
    n = s.shape[0]
    iota = lax.broadcasted_iota(jnp.int32, s.shape, 0)
    vals, idxs = [], []
    for _ in range(PEER_TOPK):
        m = jnp.max(s, axis=0, keepdims=True)
        first = jnp.min(jnp.where(s == m, iota, n), axis=0, keepdims=True)
        s = jnp.where(iota == first, _NEG_INF, s)
        vals.append(m)
        idxs.append(first.astype(F32))
    return vals, idxs


def _top16_pairs(v1, v2):
    t = v1[0].shape[1]
    v1m = jnp.concatenate(v1, axis=0)
    v2m = jnp.concatenate(v2, axis=0)
    v2s = v2m[0:_PAIR_SLAB]
    i16 = lax.broadcasted_iota(jnp.int32, (PEER_TOPK, t), 0)
    i8 = lax.broadcasted_iota(jnp.int32, (_PAIR_SLAB, t), 0)
    slabs = [v1[0] + v2m, v1[1] + v2s]
    flats = [i16, PEER_TOPK + i8]
    for p in range(2, _PAIR_SLAB):
        slabs.append(jnp.where(i8 < PEER_TOPK // (p + 1), v1[p] + v2s, _NEG_INF))
        flats.append(PEER_TOPK * p + i8)
    slabs.append(v1m[_PAIR_SLAB:] + v2[0])
    flats.append(PEER_TOPK * (_PAIR_SLAB + i8))
    c = jnp.concatenate(slabs, axis=0)
    flat = jnp.concatenate(flats, axis=0)
    big = PEER_TOPK * PEER_TOPK
    sums, picks = [], []
    for _ in range(PEER_TOPK):
        m = jnp.max(c, axis=0, keepdims=True)
        first = jnp.min(jnp.where(c == m, flat, big), axis=0, keepdims=True)
        c = jnp.where(flat == first, _NEG_INF, c)
        sums.append(m)
        picks.append(first)
    pick = jnp.concatenate(picks, axis=0)
    p_sel = jnp.right_shift(pick, 4).astype(F32)
    q_sel = jnp.bitwise_and(pick, PEER_TOPK - 1).astype(F32)
    return jnp.concatenate(sums, axis=0), p_sel, q_sel


def _lookup(ranks, rows):
    out = jnp.zeros(ranks.shape, F32)
    for p in range(PEER_TOPK):
        out = jnp.where(ranks == float(p), rows[p], out)
    return out


def _peer_score_kernel(x_ref, n2_ref, wq_ref, kblk_ref, h_ref, i1_ref, i2_ref, w_ref,
                       st_ref, a_ref, b_ref, g_ref, *, tt):
    x = x_ref[...]
    ms = jnp.mean(x * x, axis=-1, keepdims=True)
    h = (x * lax.rsqrt(ms + EPS) * n2_ref[...]).astype(BF16)
    h_ref[...] = h
    q = jnp.dot(h, wq_ref[...], preferred_element_type=F32)
    st_ref[...] = lax.dot_general(kblk_ref[...], q.astype(BF16), NT_DIMS, preferred_element_type=F32)

    def head(hd, carry):
        r1 = pl.multiple_of(hd * (2 * N_KEYS), 2 * N_KEYS)
        slot = pl.multiple_of(hd * PEER_TOPK, PEER_TOPK)
        for g in range(tt // LANES):
            cols = slice(g * LANES, (g + 1) * LANES)
            v1, i1 = _top16(st_ref[pl.ds(r1, N_KEYS), cols])
            v2, i2 = _top16(st_ref[pl.ds(r1 + N_KEYS, N_KEYS), cols])
            sums, p_sel, q_sel = _top16_pairs(v1, v2)
            e = jnp.exp(sums - sums[0:1])
            a_ref[pl.ds(slot, PEER_TOPK), cols] = _lookup(p_sel, i1)
            b_ref[pl.ds(slot, PEER_TOPK), cols] = _lookup(q_sel, i2)
            g_ref[pl.ds(slot, PEER_TOPK), cols] = e / jnp.sum(e, axis=0, keepdims=True)
        return carry

    lax.fori_loop(0, PEER_HEADS, head, 0)
    i1_ref[...] = a_ref[...].T
    i2_ref[...] = b_ref[...].T
    w_ref[...] = g_ref[...].T


def _peer_score(x1, n2, wq, kblk, tt):
    T = x1.shape[0]
    slots = jax.ShapeDtypeStruct((T, PEER_SLOTS), F32)
    sspec = pl.BlockSpec((tt, PEER_SLOTS), lambda i: (i, 0))
    return pl.pallas_call(
        functools.partial(_peer_score_kernel, tt=tt),
        out_shape=(jax.ShapeDtypeStruct((T, D_MODEL), BF16), slots, slots, slots),
        grid=(T // tt,),
        in_specs=[pl.BlockSpec((tt, D_MODEL), lambda i: (i, 0)),
                  pl.BlockSpec(n2.shape, lambda i: (0, 0)),
                  pl.BlockSpec(wq.shape, lambda i: (0, 0)),
                  pl.BlockSpec(kblk.shape, lambda i: (0, 0))],
        out_specs=(pl.BlockSpec((tt, D_MODEL), lambda i: (i, 0)), sspec, sspec, sspec),
        scratch_shapes=[pltpu.VMEM((2 * PEER_HEADS * N_KEYS, tt), F32)]
                       + [pltpu.VMEM((PEER_SLOTS, tt), F32)] * 3,
        compiler_params=pltpu.CompilerParams(dimension_semantics=("arbitrary",), vmem_limit_bytes=VMEM_LIMIT),
        name="peer_score",
    )(x1, n2, wq, kblk)


_GATE_GROUP = 2 * SUBLANES


def _peer_gate_kernel(i1_ref, i2_ref, w_ref, g_ref, *, tg):
    key = lax.broadcasted_iota(jnp.int32, (N_KEYS, PEER_SLOTS), 0).astype(F32)

    def group(gi, carry):
        t0 = pl.multiple_of(gi * _GATE_GROUP, _GATE_GROUP)
        per_token = []
        for s in range(_GATE_GROUP):
            i1 = i1_ref[pl.ds(t0 + s, 1), :]
            i2 = i2_ref[pl.ds(t0 + s, 1), :]
            w = w_ref[pl.ds(t0 + s, 1), :]
            lhs = jnp.where(key == i1, w, 0.0).astype(BF16)
            rhs = jnp.where(key == i2, 1.0, 0.0).astype(BF16)
            per_token.append(lax.dot_general(lhs, rhs, NT_DIMS, preferred_element_type=F32))
        by_key = jnp.swapaxes(jnp.stack(per_token, axis=0), 0, 1)
        for a in range(N_KEYS):
            g_ref[pl.ds(t0, _GATE_GROUP), a * N_KEYS:(a + 1) * N_KEYS] = by_key[a].astype(BF16)
        return carry

    lax.fori_loop(0, tg // _GATE_GROUP, group, 0)


def _peer_gate(i1, i2, w, tg):
    T = i1.shape[0]
    sspec = pl.BlockSpec((tg, PEER_SLOTS), lambda i: (i, 0))
    return pl.pallas_call(
        functools.partial(_peer_gate_kernel, tg=tg),
        out_shape=jax.ShapeDtypeStruct((T, N_EXPERTS), BF16),
        grid=(T // tg,),
        in_specs=[sspec, sspec, sspec],
        out_specs=pl.BlockSpec((tg, N_EXPERTS), lambda i: (i, 0)),
        compiler_params=pltpu.CompilerParams(dimension_semantics=("arbitrary",), vmem_limit_bytes=VMEM_LIMIT),
        name="peer_gate",
    )(i1, i2, w)


def _peer_expert_kernel(h_ref, ut_ref, v_ref, g_ref, x_ref, nf_ref, y_ref, acc_ref):
    e = pl.program_id(1)

    @pl.when(e == 0)
    def _():
        acc_ref[...] = jnp.zeros(acc_ref.shape, F32)

    x = jnp.dot(h_ref[...], ut_ref[...], preferred_element_type=F32)
    act = 0.5 * x * (1.0 + lax.erf(x * (2.0 ** -0.5)))
    ga = (g_ref[...].astype(F32) * act).astype(BF16)
    acc_ref[...] += jnp.dot(ga, v_ref[...], preferred_element_type=F32)

    @pl.when(e == pl.num_programs(1) - 1)
    def _():
        xo = x_ref[...] + acc_ref[...]
        ms = jnp.mean(xo * xo, axis=-1, keepdims=True)
        y_ref[...] = xo * lax.rsqrt(ms + EPS) * nf_ref[...]


def _peer_expert(h, ut, v, g, x1, nf, tt):
    T = x1.shape[0]
    n_et, _, et = ut.shape
    return pl.pallas_call(
        _peer_expert_kernel,
        out_shape=jax.ShapeDtypeStruct((T, D_MODEL), F32),
        grid=(T // tt, n_et),
        in_specs=[pl.BlockSpec((tt, D_MODEL), lambda i, e: (i, 0)),
                  pl.BlockSpec((None, D_MODEL, et), lambda i, e: (e, 0, 0)),
                  pl.BlockSpec((et, D_MODEL), lambda i, e: (e, 0)),
                  pl.BlockSpec((tt, et), lambda i, e: (i, e)),
                  pl.BlockSpec((tt, D_MODEL), lambda i, e: (i, 0), pipeline_mode=pl.Buffered(1)),
                  pl.BlockSpec(nf.shape, lambda i, e: (0, 0))],
        out_specs=pl.BlockSpec((tt, D_MODEL), lambda i, e: (i, 0)),
        scratch_shapes=[pltpu.VMEM((tt, D_MODEL), F32)],
        compiler_params=pltpu.CompilerParams(dimension_semantics=("arbitrary", "arbitrary"),
                                             vmem_limit_bytes=VMEM_LIMIT),
        name="peer_expert",
    )(h, ut, v, g, x1, nf)


def _pick_tile(T, prefs):
    for t in prefs:
        if T % t == 0:
            return t
    raise ValueError(f"no tile for {T}")


def _layer(x, conv_prev, ssm_prev, k_cache, v_cache, lam, lam_init, wts):
    (n1, w_main, w_ba, w_bat, conv_w, arow, drow, acol, dcol, gnw, sw, w_out, n2, wq, kblk, ut, v, nf) = wts
    B, T, _ = x.shape
    n_tok = B * T
    x2d = x.reshape(n_tok, D_MODEL)
    tm = _pick_tile(n_tok, (256, 128))
    conv_in, z, qb, kb, vb, kbh, vbh, ba, bat = _inproj(x2d, n1, w_main, w_ba, w_bat, tm)

    C = min(CHUNK, T)
    prev8 = jnp.pad(conv_prev.astype(F32), ((0, 0), (SUBLANES - CONV_W + 1, 0), (0, 0)))
    bat4 = bat.reshape(SUBLANES, B, T // C, C).transpose(1, 2, 0, 3)
    o_a, ssm_new = _gdn(conv_in.reshape(B, T, CONV_CH), z.reshape(B, T, W_A), ba.reshape(B, T, LANES), bat4,
                        prev8, ssm_prev.astype(F32), conv_w, arow, drow, acol, dcol, gnw, C,
                        2 if (T // C) % 2 == 0 else 1)
    conv_new = conv_in.reshape(B, T, CONV_CH)[:, T - (CONV_W - 1):, :]

    if k_cache is None:
        tq = _pick_tile(T, (512, 256, 128))
        o_b = _attn_prompt(lam, qb, kbh, vbh, sw, tq, lam_init)
    else:
        P = k_cache.shape[1]
        o_b = _attn_sample(lam, qb.reshape(B, T, W_B), kbh.reshape(B, T, W_B), vbh.reshape(B, T, W_B),
                           k_cache.reshape(B, P, W_B), v_cache.reshape(B, P, W_B), sw, lam_init)
        o_b = o_b.reshape(n_tok, W_B)

    x1 = _outproj(x2d, o_a.reshape(n_tok, W_A), o_b, w_out, tm)

    tt = _pick_tile(n_tok, (256, 128))
    h2, i1, i2, gate = _peer_score(x1, n2, wq, kblk, tt)
    g = _peer_gate(i1, i2, gate, _pick_tile(n_tok, (128,)))
    y = _peer_expert(h2, ut, v, g, x1, nf, _pick_tile(n_tok, (1024, 512, 256, 128)))
    return (y.reshape(B, T, D_MODEL), kb.reshape(B, T, H_B, 2 * DH_B), vb.reshape(B, T, H_B, DV_B),
            conv_new, ssm_new)


def _prep_weights(norm1_w, w_in, conv_w, a_log, dt_bias, gdn_norm_w, subln_w, w_out,
                  norm2_w, peer_wq, peer_k1, peer_k2, peer_u, peer_v, norm_f_w):
    o_beta = CONV_CH + W_A
    w_main = jnp.concatenate([w_in[:, :o_beta], w_in[:, o_beta + 2 * H_A:]], axis=1).astype(BF16)
    w_gate = w_in[:, o_beta:o_beta + 2 * H_A]
    w_ba = jnp.pad(w_gate, ((0, 0), (0, LANES - 2 * H_A))).astype(BF16)
    w_bat = w_gate.T.astype(BF16)
    zeros4 = jnp.zeros((H_A,), F32)
    a8 = jnp.concatenate([zeros4, a_log.astype(F32)])
    d8 = jnp.concatenate([zeros4, dt_bias.astype(F32)])
    arow = jnp.pad(a8, (0, LANES - 2 * H_A)).reshape(1, LANES)
    drow = jnp.pad(d8, (0, LANES - 2 * H_A)).reshape(1, LANES)
    acol = a8.reshape(2 * H_A, 1)
    dcol = d8.reshape(2 * H_A, 1)
    halves = jnp.stack([peer_k1, peer_k2]).astype(F32)
    eye = jnp.eye(2 * PEER_HEADS, dtype=F32)
    kblk = jnp.einsum("gj,gnd->gnjd", eye, jnp.tile(halves, (PEER_HEADS, 1, 1)))
    kblk = kblk.reshape(2 * PEER_HEADS * N_KEYS, D_MODEL).astype(BF16)
    et = PEER_NA * N_KEYS
    return (norm1_w.reshape(1, D_MODEL), w_main, w_ba, w_bat, conv_w, arow, drow, acol, dcol,
            gdn_norm_w.reshape(1, DK_A), subln_w.reshape(1, DV_B), w_out.astype(BF16),
            norm2_w.reshape(1, D_MODEL), peer_wq.astype(BF16), kblk,
            peer_u.reshape(N_EXPERTS // et, et, D_MODEL).transpose(0, 2, 1).astype(BF16),
            peer_v.astype(BF16), norm_f_w.reshape(1, D_MODEL))


def kernel(x_prompt, x_sample, cache_k, cache_v, state_conv, state_gdn, norm1_w, w_in, conv_w, a_log, dt_bias, gdn_norm_w, lam_q1, lam_k1, lam_q2, lam_k2, subln_w, w_out, norm2_w, peer_wq, peer_k1, peer_k2, peer_u, peer_v, norm_f_w):
    depth = w_in.shape[0]
    assert depth == 1, "the final norm is fused into the (single) layer's last kernel"
    l = 0
    lam_init = 0.8 - 0.6 * math.exp(-0.3 * l)
    lam = (jnp.exp(jnp.sum(lam_q1[l].astype(F32) * lam_k1[l].astype(F32)))
           - jnp.exp(jnp.sum(lam_q2[l].astype(F32) * lam_k2[l].astype(F32))) + lam_init).reshape(1)
    layer = lambda a: a.reshape(a.shape[1:])
    wts = _prep_weights(layer(norm1_w), layer(w_in), layer(conv_w), layer(a_log), layer(dt_bias),
                        layer(gdn_norm_w), layer(subln_w), layer(w_out), layer(norm2_w), layer(peer_wq),
                        layer(peer_k1), layer(peer_k2), layer(peer_u), layer(peer_v), norm_f_w)
    bp = x_prompt.shape[0]
    conv0 = jnp.zeros((bp, CONV_W - 1, CONV_CH), F32)
    ssm0 = jnp.zeros((bp, H_A, DK_A, DK_A), F32)
    yp, k1, v1, c1, s1 = _layer(x_prompt, conv0, ssm0, None, None, lam, lam_init, wts)
    ys, k2, v2, c2, s2 = _layer(x_sample, layer(state_conv), layer(state_gdn), layer(cache_k), layer(cache_v),
                                lam, lam_init, wts)
    st = lambda a: a[None]
    return (yp, ys, st(k1), st(v1), st(c1), st(s1), st(k2), st(v2), st(c2), st(s2))
```

```python
import functools
import math

import jax
import jax.numpy as jnp
from jax import lax
from jax.experimental import pallas as pl
from jax.experimental.pallas import tpu as pltpu

F32 = jnp.float32
BF16 = jnp.bfloat16
HIGHEST = lax.Precision.HIGHEST

D_MODEL = 1024
CHUNK = 64
H_A = 4
DK_A = 128
CONV_W = 4
W_A = H_A * DK_A
CONV_CH = 3 * W_A
H_B = 4
DH_B = 64
DV_B = 128
W_B = H_B * DV_B
DIFF_EPS = 1e-5
N_KEYS = 128
N_EXPERTS = N_KEYS * N_KEYS
PEER_HEADS = 8
PEER_TOPK = 16
PEER_DK_HALF = 64
PEER_NA = 8
EPS = 1e-6
LANES = 128
SUBLANES = 8
VMEM_LIMIT = 56 * 1024 * 1024

NT_DIMS = (((1,), (1,)), ((), ()))
TN_DIMS = (((0,), (0,)), ((), ()))


def _sigmoid(x):
    return 1.0 / (1.0 + jnp.exp(-x))


def _softplus(x):
    return jnp.maximum(x, 0.0) + jnp.log(1.0 + jnp.exp(-jnp.abs(x)))


def _dot_hi(a, b):
    return jnp.dot(a, b, precision=HIGHEST, preferred_element_type=F32)


def _dot_bf(a, b, dims=(((1,), (0,)), ((), ()))):
    return lax.dot_general(a.astype(BF16), b.astype(BF16), dims, preferred_element_type=F32)


def _split_bf(a):
    hi = a.astype(BF16)
    return hi, (a - hi.astype(F32)).astype(BF16)


def _dot_split(a, b):
    d = lambda x, y: jnp.dot(x, y, preferred_element_type=F32)
    return d(a[0], b[0]) + (d(a[0], b[1]) + d(a[1], b[0]))


def _inproj_kernel(x_ref, n1_ref, w_ref, wba_ref, wbat_ref,
                   conv_ref, z_ref, qb_ref, kb_ref, vb_ref, kbh_ref, vbh_ref, ba_ref, bat_ref):
    x = x_ref[...]
    ms = jnp.mean(x * x, axis=-1, keepdims=True)
    h = (x * lax.rsqrt(ms + EPS) * n1_ref[...]).astype(BF16)

    def mm(c0, c1):
        return jnp.dot(h, w_ref[:, c0:c1], preferred_element_type=F32)

    conv_ref[...] = mm(0, CONV_CH)
    z_ref[...] = mm(CONV_CH, CONV_CH + W_A)
    o = CONV_CH + W_A
    qb_ref[...] = (mm(o, o + W_B) * (DH_B ** -0.5)).astype(BF16)
    kb = mm(o + W_B, o + 2 * W_B)
    kb_ref[...] = kb
    kbh_ref[...] = kb.astype(BF16)
    vb = mm(o + 2 * W_B, o + 3 * W_B)
    vb_ref[...] = vb
    vbh_ref[...] = vb.astype(BF16)
    ba_ref[...] = jnp.dot(h, wba_ref[...], preferred_element_type=F32)
    bat_ref[...] = lax.dot_general(wbat_ref[...], h, NT_DIMS, preferred_element_type=F32)


def _inproj(x2d, n1, w_main, w_ba, w_bat, tm):
    T = x2d.shape[0]
    row = lambda c: pl.BlockSpec((tm, c), lambda i: (i, 0))
    full = lambda a: pl.BlockSpec(a.shape, lambda i: (0, 0))
    out_shape = (
        jax.ShapeDtypeStruct((T, CONV_CH), F32),
        jax.ShapeDtypeStruct((T, W_A), F32),
        jax.ShapeDtypeStruct((T, W_B), BF16),
        jax.ShapeDtypeStruct((T, W_B), F32),
        jax.ShapeDtypeStruct((T, W_B), F32),
        jax.ShapeDtypeStruct((T, W_B), BF16),
        jax.ShapeDtypeStruct((T, W_B), BF16),
        jax.ShapeDtypeStruct((T, LANES), F32),
        jax.ShapeDtypeStruct((SUBLANES, T), F32),
    )
    out_specs = (row(CONV_CH), row(W_A), row(W_B), row(W_B), row(W_B), row(W_B), row(W_B), row(LANES),
                 pl.BlockSpec((SUBLANES, tm), lambda i: (0, i)))
    return pl.pallas_call(
        _inproj_kernel,
        out_shape=out_shape,
        grid=(T // tm,),
        in_specs=[row(D_MODEL), full(n1), full(w_main), full(w_ba), full(w_bat)],
        out_specs=out_specs,
        compiler_params=pltpu.CompilerParams(dimension_semantics=("arbitrary",), vmem_limit_bytes=VMEM_LIMIT),
        name="inproj",
    )(x2d, n1, w_main, w_ba, w_bat)


def _gdn_kernel(conv_ref, z_ref, ba_ref, bat_ref, prev_ref, s0_ref, cw_ref, arow_ref, drow_ref,
                acol_ref, dcol_ref, nw_ref, o_ref, sout_ref, up_ref, s_ref, *, C, nc):
    t = pl.program_id(1)
    tt = nc * C

    @pl.when(t == 0)
    def _():
        up_ref[0:SUBLANES, :] = prev_ref[...]
        s_ref[...] = s0_ref[...]

    up_ref[SUBLANES:SUBLANES + tt, :] = conv_ref[...]
    y = up_ref[pl.ds(SUBLANES - CONV_W + 1, tt), :] * cw_ref[0:1, :]
    for j in range(1, CONV_W):
        y = y + up_ref[pl.ds(SUBLANES - CONV_W + 1 + j, tt), :] * cw_ref[j:j + 1, :]
    qkv = y * _sigmoid(y)
    tail = up_ref[tt:tt + SUBLANES, :]
    up_ref[0:SUBLANES, :] = tail

    ri = lax.broadcasted_iota(jnp.int32, (C, C), 0)
    ci = lax.broadcasted_iota(jnp.int32, (C, C), 1)
    incl = ri >= ci
    strict = ri > ci
    tril = incl.astype(F32)
    triu = (ri <= ci).astype(F32)
    eye = (ri == ci).astype(F32)

    ba = ba_ref[...]
    beta_all = _sigmoid(ba)
    g_all = -jnp.exp(arow_ref[...]) * _softplus(ba + drow_ref[...])
    nw = nw_ref[...]

    probs = []
    for c in range(nc):
        rows = slice(c * C, (c + 1) * C)
        gc_all = _dot_hi(tril, g_all[rows])
        g_t = -jnp.exp(acol_ref[...]) * _softplus(bat_ref[c] + dcol_ref[...])
        gc_t = _dot_hi(g_t, triu)
        for h in range(H_A):
            gcc = gc_all[:, H_A + h:H_A + h + 1]
            gcr = gc_t[H_A + h:H_A + h + 1, :]
            beta = beta_all[rows, h:h + 1]
            decay = jnp.where(incl, jnp.exp(jnp.where(incl, gcc - gcr, 0.0)), 0.0)
            qh = qkv[rows, h * DK_A:(h + 1) * DK_A]
            kh = qkv[rows, W_A + h * DK_A:W_A + (h + 1) * DK_A]
            vh = qkv[rows, 2 * W_A + h * DK_A:2 * W_A + (h + 1) * DK_A]
            qh = qh * lax.rsqrt(jnp.sum(qh * qh, axis=-1, keepdims=True) + EPS) * (DK_A ** -0.5)
            kh = kh * lax.rsqrt(jnp.sum(kh * kh, axis=-1, keepdims=True) + EPS)
            kb = kh * beta
            n = jnp.where(strict, -(_dot_bf(kb, kh, NT_DIMS) * decay), 0.0)
            egc = jnp.exp(gcc)
            g_last = gcc[C - 1:C, :]
            probs.append(dict(h=h, rows=rows, n=n, vb=(vh * beta).astype(BF16),
                              kbe=(kb * egc).astype(BF16), qg=(qh * egc).astype(BF16),
                              attn=(_dot_bf(qh, kh, NT_DIMS) * decay).astype(BF16),
                              kd=(kh * jnp.exp(g_last - gcc)).astype(BF16), eg=jnp.exp(g_last)))

    tinv = [eye + pr["n"] for pr in probs]
    pw = [_split_bf(pr["n"]) for pr in probs]
    for _ in range(int(math.log2(C)) - 1):
        pw = [_split_bf(_dot_split(p, p)) for p in pw]
        tinv = [ti + _dot_split(_split_bf(ti), p) for ti, p in zip(tinv, pw)]
    for pr, ti in zip(probs, tinv):
        ti = ti.astype(BF16)
        pr["u"] = jnp.dot(ti, pr["vb"], preferred_element_type=F32)
        pr["w"] = jnp.dot(ti, pr["kbe"], preferred_element_type=F32).astype(BF16)

    state = [s_ref[h] for h in range(H_A)]
    for pr in probs:
        h, rows = pr["h"], pr["rows"]
        s = state[h]
        s_bf = s.astype(BF16)
        v_new = (pr["u"] - jnp.dot(pr["w"], s_bf, preferred_element_type=F32)).astype(BF16)
        o = (jnp.dot(pr["qg"], s_bf, preferred_element_type=F32)
             + jnp.dot(pr["attn"], v_new, preferred_element_type=F32))
        state[h] = s * pr["eg"] + lax.dot_general(pr["kd"], v_new, TN_DIMS, preferred_element_type=F32)
        on = o * lax.rsqrt(jnp.mean(o * o, axis=-1, keepdims=True) + EPS) * nw
        zz = z_ref[rows, h * DK_A:(h + 1) * DK_A]
        o_ref[rows, h * DK_A:(h + 1) * DK_A] = (on * (zz * _sigmoid(zz))).astype(o_ref.dtype)
    for h in range(H_A):
        s_ref[h] = state[h]

    @pl.when(t == pl.num_programs(1) - 1)
    def _():
        sout_ref[...] = s_ref[...]


def _gdn(conv_in, z, ba, bat, prev8, s0, conv_w, arow, drow, acol, dcol, nw, C, nc):
    B, T, _ = conv_in.shape
    tt = nc * C
    tok = lambda c: pl.BlockSpec((None, tt, c), lambda b, t: (b, t, 0))
    full2 = lambda a: pl.BlockSpec(a.shape, lambda b, t: (0, 0))
    return pl.pallas_call(
        functools.partial(_gdn_kernel, C=C, nc=nc),
        out_shape=(jax.ShapeDtypeStruct((B, T, W_A), BF16),
                   jax.ShapeDtypeStruct((B, H_A, DK_A, DK_A), F32)),
        grid=(B, T // tt),
        in_specs=[tok(CONV_CH), tok(W_A), tok(LANES),
                  pl.BlockSpec((None, nc, SUBLANES, C), lambda b, t: (b, t, 0, 0)),
                  pl.BlockSpec((None, SUBLANES, CONV_CH), lambda b, t: (b, 0, 0)),
                  pl.BlockSpec((None, H_A, DK_A, DK_A), lambda b, t: (b, 0, 0, 0)),
                  full2(conv_w), full2(arow), full2(drow), full2(acol), full2(dcol), full2(nw)],
        out_specs=(tok(W_A), pl.BlockSpec((None, H_A, DK_A, DK_A), lambda b, t: (b, 0, 0, 0))),
        scratch_shapes=[pltpu.VMEM((SUBLANES + tt, CONV_CH), F32), pltpu.VMEM((H_A, DK_A, DK_A), F32)],
        compiler_params=pltpu.CompilerParams(dimension_semantics=("arbitrary", "arbitrary"),
                                             vmem_limit_bytes=VMEM_LIMIT),
        name="gdn",
    )(conv_in, z, ba, bat, prev8, s0, conv_w, arow, drow, acol, dcol, nw)


def _split_maps(q):
    lane = lax.broadcasted_iota(jnp.int32, q.shape, 1)
    zero = jnp.zeros_like(q)
    return jnp.concatenate([jnp.where(lane < DH_B, q, zero), jnp.where(lane >= DH_B, q, zero)], axis=0)


def _subln(acc, l, lam, sw, tq, lam_init):
    o = acc[:tq] / l[:tq] - lam * (acc[tq:] / l[tq:])
    return o * lax.rsqrt(jnp.mean(o * o, axis=-1, keepdims=True) + DIFF_EPS) * sw * (1.0 - lam_init)


def _attn_prompt_kernel(lam_ref, q_ref, k_ref, v_ref, sw_ref, o_ref, qs_ref, s0_ref, s1_ref, m_ref, l_ref,
                        acc_ref, *, tq, rc, lam_init):
    qi = pl.program_id(1)
    qs_ref[...] = _split_maps(q_ref[...])
    m_ref[...] = jnp.full(m_ref.shape, -jnp.inf, F32)
    l_ref[...] = jnp.zeros(l_ref.shape, F32)
    acc_ref[...] = jnp.zeros(acc_ref.shape, F32)
    chunks = [slice(c * rc, (c + 1) * rc) for c in range(2 * tq // rc)]
    rep = tq // LANES

    def scores(j, s_ref):
        kblk = k_ref[pl.ds(pl.multiple_of(j * tq, tq), tq), :]
        for rows in chunks:
            s_ref[rows, :] = lax.dot_general(qs_ref[rows, :], kblk, NT_DIMS, preferred_element_type=F32)

    def softmax_pv(j, s_ref, masked):
        vblk = v_ref[pl.ds(pl.multiple_of(j * tq, tq), tq), :]
        for rows in chunks:
            s = s_ref[rows, :]
            if masked:
                r = lax.broadcasted_iota(jnp.int32, s.shape, 0) + (rows.start % tq)
                c = lax.broadcasted_iota(jnp.int32, s.shape, 1)
                s = jnp.where((c // CHUNK) <= (r // CHUNK), s, -jnp.inf)
            m_old = m_ref[rows, :]
            m_new = jnp.maximum(m_old, jnp.max(s, axis=-1, keepdims=True))
            alpha = jnp.exp(m_old - m_new)
            p = jnp.exp(s - jnp.concatenate([m_new] * rep, axis=1))
            l_ref[rows, :] = alpha * l_ref[rows, :] + jnp.sum(p, axis=-1, keepdims=True)
            acc_ref[rows, :] = alpha * acc_ref[rows, :] + jnp.dot(p.astype(BF16), vblk,
                                                                  preferred_element_type=F32)
            m_ref[rows, :] = m_new

    scores(qi, s0_ref)
    softmax_pv(qi, s0_ref, True)

    @pl.when(qi > 0)
    def _():
        scores(0, s0_ref)

        def pair(i, carry):
            j = 2 * i
            scores(j + 1, s1_ref)
            softmax_pv(j, s0_ref, False)
            scores(j + 2, s0_ref)
            softmax_pv(j + 1, s1_ref, False)
            return carry

        lax.fori_loop(0, qi // 2, pair, 0)

        @pl.when(qi % 2 == 1)
        def _():
            softmax_pv(qi - 1, s0_ref, False)

    o_ref[...] = _subln(acc_ref[...], l_ref[...], lam_ref[0], sw_ref[...], tq, lam_init).astype(o_ref.dtype)


def _attn_prompt(lam, q, k, v, sw, tq, lam_init):
    T = q.shape[0]
    return pl.pallas_call(
        functools.partial(_attn_prompt_kernel, tq=tq, rc=min(512, tq), lam_init=lam_init),
        out_shape=jax.ShapeDtypeStruct((T, W_B), BF16),
        grid=(H_B, T // tq),
        in_specs=[pl.BlockSpec(memory_space=pltpu.SMEM),
                  pl.BlockSpec((tq, DV_B), lambda h, i: (i, h)),
                  pl.BlockSpec((T, DV_B), lambda h, i: (0, h)),
                  pl.BlockSpec((T, DV_B), lambda h, i: (0, h)),
                  pl.BlockSpec((1, DV_B), lambda h, i: (0, 0))],
        out_specs=pl.BlockSpec((tq, DV_B), lambda h, i: (i, h)),
        scratch_shapes=[pltpu.VMEM((2 * tq, DV_B), BF16), pltpu.VMEM((2 * tq, tq), F32),
                        pltpu.VMEM((2 * tq, tq), F32),
                        pltpu.VMEM((2 * tq, LANES), F32), pltpu.VMEM((2 * tq, LANES), F32),
                        pltpu.VMEM((2 * tq, DV_B), F32)],
        compiler_params=pltpu.CompilerParams(dimension_semantics=("arbitrary", "arbitrary"),
                                             vmem_limit_bytes=VMEM_LIMIT),
        name="attn_prompt",
    )(lam, q, k, v, sw)


def _attn_sample_kernel(lam_ref, q_ref, kn_ref, vn_ref, ck_ref, cv_ref, sw_ref, o_ref, *, tq, lam_init):
    qs = _split_maps(q_ref[...])
    kc = ck_ref[...].astype(BF16)
    vc = cv_ref[...].astype(BF16)
    sc = lax.dot_general(qs, kc, NT_DIMS, preferred_element_type=F32)
    sn = lax.dot_general(qs, kn_ref[...], NT_DIMS, preferred_element_type=F32)
    m = jnp.maximum(jnp.max(sc, axis=-1, keepdims=True), jnp.max(sn, axis=-1, keepdims=True))
    pc = jnp.exp(sc - m)
    pn = jnp.exp(sn - m)
    l = jnp.sum(pc, axis=-1, keepdims=True) + jnp.sum(pn, axis=-1, keepdims=True)
    acc = (jnp.dot(pc.astype(BF16), vc, preferred_element_type=F32)
           + jnp.dot(pn.astype(BF16), vn_ref[...], preferred_element_type=F32))
    o_ref[...] = _subln(acc, l, lam_ref[0], sw_ref[...], tq, lam_init).astype(o_ref.dtype)


def _attn_sample(lam, q, kn, vn, ck, cv, sw, lam_init):
    B, tq, _ = q.shape
    P = ck.shape[1]
    new = pl.BlockSpec((None, tq, DV_B), lambda b, h: (b, 0, h))
    cache = pl.BlockSpec((None, P, DV_B), lambda b, h: (b, 0, h))
    return pl.pallas_call(
        functools.partial(_attn_sample_kernel, tq=tq, lam_init=lam_init),
        out_shape=jax.ShapeDtypeStruct((B, tq, W_B), BF16),
        grid=(B, H_B),
        in_specs=[pl.BlockSpec(memory_space=pltpu.SMEM), new, new, new, cache, cache,
                  pl.BlockSpec((1, DV_B), lambda b, h: (0, 0))],
        out_specs=new,
        compiler_params=pltpu.CompilerParams(dimension_semantics=("arbitrary", "arbitrary"),
                                             vmem_limit_bytes=VMEM_LIMIT),
        name="attn_sample",
    )(lam, q, kn, vn, ck, cv, sw)


def _outproj_kernel(x_ref, oa_ref, ob_ref, w_ref, y_ref):
    y_ref[...] = (x_ref[...]
                  + jnp.dot(oa_ref[...], w_ref[0:W_A, :], preferred_element_type=F32)
                  + jnp.dot(ob_ref[...], w_ref[W_A:W_A + W_B, :], preferred_element_type=F32))


def _outproj(x2d, oa, ob, w_out, tm):
    T = x2d.shape[0]
    row = lambda c: pl.BlockSpec((tm, c), lambda i: (i, 0))
    return pl.pallas_call(
        _outproj_kernel,
        out_shape=jax.ShapeDtypeStruct((T, D_MODEL), F32),
        grid=(T // tm,),
        in_specs=[row(D_MODEL), row(W_A), row(W_B), pl.BlockSpec(w_out.shape, lambda i: (0, 0))],
        out_specs=row(D_MODEL),
        compiler_params=pltpu.CompilerParams(dimension_semantics=("arbitrary",), vmem_limit_bytes=VMEM_LIMIT),
        name="outproj",
    )(x2d, oa, ob, w_out)


_NEG_INF = float("-inf")
_PAIR_SLAB = SUBLANES
PEER_SLOTS = PEER_HEADS * PEER_TOPK


def _top16(s):
    n = s.shape[0]
    iota = lax.broadcasted_iota(jnp.int32, s.shape, 0)
    vals, idxs = [], []
    for _ in range(PEER_TOPK):
        m = jnp.max(s, axis=0, keepdims=True)
        first = jnp.min(jnp.where(s == m, iota, n), axis=0, keepdims=True)
        s = jnp.where(iota == first, _NEG_INF, s)
        vals.append(m)
        idxs.append(first.astype(F32))
    return vals, idxs


def _top16_pairs(v1, v2):
    t = v1[0].shape[1]
    v1m = jnp.concatenate(v1, axis=0)
    v2m = jnp.concatenate(v2, axis=0)
    v2s = v2m[0:_PAIR_SLAB]
    i16 = lax.broadcasted_iota(jnp.int32, (PEER_TOPK, t), 0)
    i8 = lax.broadcasted_iota(jnp.int32, (_PAIR_SLAB, t), 0)
    slabs = [v1[0] + v2m, v1[1] + v2s]
    flats = [i16, PEER_TOPK + i8]
    for p in range(2, _PAIR_SLAB):
        slabs.append(jnp.where(i8 < PEER_TOPK // (p + 1), v1[p] + v2s, _NEG_INF))
        flats.append(PEER_TOPK * p + i8)
    slabs.append(v1m[_PAIR_SLAB:] + v2[0])
    flats.append(PEER_TOPK * (_PAIR_SLAB + i8))
    c = jnp.concatenate(slabs, axis=0)
    flat = jnp.concatenate(flats, axis=0)
    big = PEER_TOPK * PEER_TOPK
    sums, picks = [], []
    for _ in range(PEER_TOPK):
        m = jnp.max(c, axis=0, keepdims=True)
        first = jnp.min(jnp.where(c == m, flat, big), axis=0, keepdims=True)
        c = jnp.where(flat == first, _NEG_INF, c)
        sums.append(m)
        picks.append(first)
    pick = jnp.concatenate(picks, axis=0)
    p_sel = jnp.right_shift(pick, 4).astype(F32)
    q_sel = jnp.bitwise_and(pick, PEER_TOPK - 1).astype(F32)
    return jnp.concatenate(sums, axis=0), p_sel, q_sel


def _lookup(ranks, rows):
    out = jnp.zeros(ranks.shape, F32)
    for p in range(PEER_TOPK):
        out = jnp.where(ranks == float(p), rows[p], out)
    return out


def _peer_score_kernel(x_ref, n2_ref, wq_ref, kblk_ref, h_ref, i1_ref, i2_ref, w_ref,
                       st_ref, a_ref, b_ref, g_ref, *, tt):
    x = x_ref[...]
    ms = jnp.mean(x * x, axis=-1, keepdims=True)
    h = (x * lax.rsqrt(ms + EPS) * n2_ref[...]).astype(BF16)
    h_ref[...] = h
    q = jnp.dot(h, wq_ref[...], preferred_element_type=F32)
    st_ref[...] = lax.dot_general(kblk_ref[...], q.astype(BF16), NT_DIMS, preferred_element_type=F32)

    def head(hd, carry):
        r1 = pl.multiple_of(hd * (2 * N_KEYS), 2 * N_KEYS)
        slot = pl.multiple_of(hd * PEER_TOPK, PEER_TOPK)
        for g in range(tt // LANES):
            cols = slice(g * LANES, (g + 1) * LANES)
            v1, i1 = _top16(st_ref[pl.ds(r1, N_KEYS), cols])
            v2, i2 = _top16(st_ref[pl.ds(r1 + N_KEYS, N_KEYS), cols])
            sums, p_sel, q_sel = _top16_pairs(v1, v2)
            e = jnp.exp(sums - sums[0:1])
            a_ref[pl.ds(slot, PEER_TOPK), cols] = _lookup(p_sel, i1)
            b_ref[pl.ds(slot, PEER_TOPK), cols] = _lookup(q_sel, i2)
            g_ref[pl.ds(slot, PEER_TOPK), cols] = e / jnp.sum(e, axis=0, keepdims=True)
        return carry

    lax.fori_loop(0, PEER_HEADS, head, 0)
    i1_ref[...] = a_ref[...].T
    i2_ref[...] = b_ref[...].T
    w_ref[...] = g_ref[...].T


def _peer_score(x1, n2, wq, kblk, tt):
    T = x1.shape[0]
    slots = jax.ShapeDtypeStruct((T, PEER_SLOTS), F32)
    sspec = pl.BlockSpec((tt, PEER_SLOTS), lambda i: (i, 0))
    return pl.pallas_call(
        functools.partial(_peer_score_kernel, tt=tt),
        out_shape=(jax.ShapeDtypeStruct((T, D_MODEL), BF16), slots, slots, slots),
        grid=(T // tt,),
        in_specs=[pl.BlockSpec((tt, D_MODEL), lambda i: (i, 0)),
                  pl.BlockSpec(n2.shape, lambda i: (0, 0)),
                  pl.BlockSpec(wq.shape, lambda i: (0, 0)),
                  pl.BlockSpec(kblk.shape, lambda i: (0, 0))],
        out_specs=(pl.BlockSpec((tt, D_MODEL), lambda i: (i, 0)), sspec, sspec, sspec),
        scratch_shapes=[pltpu.VMEM((2 * PEER_HEADS * N_KEYS, tt), F32)]
                       + [pltpu.VMEM((PEER_SLOTS, tt), F32)] * 3,
        compiler_params=pltpu.CompilerParams(dimension_semantics=("arbitrary",), vmem_limit_bytes=VMEM_LIMIT),
        name="peer_score",
    )(x1, n2, wq, kblk)


_GATE_GROUP = 2 * SUBLANES


def _peer_gate_kernel(i1_ref, i2_ref, w_ref, g_ref, *, tg):
    key = lax.broadcasted_iota(jnp.int32, (N_KEYS, PEER_SLOTS), 0).astype(F32)

    def group(gi, carry):
        t0 = pl.multiple_of(gi * _GATE_GROUP, _GATE_GROUP)
        per_token = []
        for s in range(_GATE_GROUP):
            i1 = i1_ref[pl.ds(t0 + s, 1), :]
            i2 = i2_ref[pl.ds(t0 + s, 1), :]
            w = w_ref[pl.ds(t0 + s, 1), :]
            lhs = jnp.where(key == i1, w, 0.0).astype(BF16)
            rhs = jnp.where(key == i2, 1.0, 0.0).astype(BF16)
            per_token.append(lax.dot_general(lhs, rhs, NT_DIMS, preferred_element_type=F32))
        by_key = jnp.swapaxes(jnp.stack(per_token, axis=0), 0, 1)
        for a in range(N_KEYS):
            g_ref[pl.ds(t0, _GATE_GROUP), a * N_KEYS:(a + 1) * N_KEYS] = by_key[a].astype(BF16)
        return carry

    lax.fori_loop(0, tg // _GATE_GROUP, group, 0)


def _peer_gate(i1, i2, w, tg):
    T = i1.shape[0]
    sspec = pl.BlockSpec((tg, PEER_SLOTS), lambda i: (i, 0))
    return pl.pallas_call(
        functools.partial(_peer_gate_kernel, tg=tg),
        out_shape=jax.ShapeDtypeStruct((T, N_EXPERTS), BF16),
        grid=(T // tg,),
        in_specs=[sspec, sspec, sspec],
        out_specs=pl.BlockSpec((tg, N_EXPERTS), lambda i: (i, 0)),
        compiler_params=pltpu.CompilerParams(dimension_semantics=("arbitrary",), vmem_limit_bytes=VMEM_LIMIT),
        name="peer_gate",
    )(i1, i2, w)


def _peer_expert_kernel(h_ref, ut_ref, v_ref, g_ref, x_ref, nf_ref, y_ref, acc_ref):
    e = pl.program_id(1)

    @pl.when(e == 0)
    def _():
        acc_ref[...] = jnp.zeros(acc_ref.shape, F32)

    x = jnp.dot(h_ref[...], ut_ref[...], preferred_element_type=F32)
    act = 0.5 * x * (1.0 + lax.erf(x * (2.0 ** -0.5)))
    ga = (g_ref[...].astype(F32) * act).astype(BF16)
    acc_ref[...] += jnp.dot(ga, v_ref[...], preferred_element_type=F32)

    @pl.when(e == pl.num_programs(1) - 1)
    def _():
        xo = x_ref[...] + acc_ref[...]
        ms = jnp.mean(xo * xo, axis=-1, keepdims=True)
        y_ref[...] = xo * lax.rsqrt(ms + EPS) * nf_ref[...]


def _peer_expert(h, ut, v, g, x1, nf, tt, et):
    T = x1.shape[0]
    return pl.pallas_call(
        _peer_expert_kernel,
        out_shape=jax.ShapeDtypeStruct((T, D_MODEL), F32),
        grid=(T // tt, N_EXPERTS // et),
        in_specs=[pl.BlockSpec((tt, D_MODEL), lambda i, e: (i, 0)),
                  pl.BlockSpec((None, D_MODEL, et), lambda i, e: (0, 0, e)),
                  pl.BlockSpec((None, et, D_MODEL), lambda i, e: (0, e, 0)),
                  pl.BlockSpec((tt, et), lambda i, e: (i, e)),
                  pl.BlockSpec((tt, D_MODEL), lambda i, e: (i, 0), pipeline_mode=pl.Buffered(1)),
                  pl.BlockSpec(nf.shape, lambda i, e: (0, 0))],
        out_specs=pl.BlockSpec((tt, D_MODEL), lambda i, e: (i, 0)),
        scratch_shapes=[pltpu.VMEM((tt, D_MODEL), F32)],
        compiler_params=pltpu.CompilerParams(dimension_semantics=("arbitrary", "arbitrary"),
                                             vmem_limit_bytes=VMEM_LIMIT),
        name="peer_expert",
    )(h, ut, v, g, x1, nf)


def _pick_tile(T, prefs):
    for t in prefs:
        if T % t == 0:
            return t
    raise ValueError(f"no tile for {T}")


def _layer(x, conv_prev, ssm_prev, k_cache, v_cache, lam, lam_init, wts):
    (n1, w_main, w_ba, w_bat, conv_w, arow, drow, acol, dcol, gnw, sw, w_out, n2, wq, kblk, ut, v, nf) = wts
    B, T, _ = x.shape
    n_tok = B * T
    x2d = x.reshape(n_tok, D_MODEL)
    tm = _pick_tile(n_tok, (256, 128))
    conv_in, z, qb, kb, vb, kbh, vbh, ba, bat = _inproj(x2d, n1, w_main, w_ba, w_bat, tm)

    C = min(CHUNK, T)
    prev8 = jnp.pad(conv_prev.astype(F32), ((0, 0), (SUBLANES - CONV_W + 1, 0), (0, 0)))
    bat4 = bat.reshape(SUBLANES, B, T // C, C).transpose(1, 2, 0, 3)
    o_a, ssm_new = _gdn(conv_in.reshape(B, T, CONV_CH), z.reshape(B, T, W_A), ba.reshape(B, T, LANES), bat4,
                        prev8, ssm_prev.astype(F32), conv_w, arow, drow, acol, dcol, gnw, C,
                        _pick_tile(T // C, (4, 2, 1)))
    conv_new = conv_in.reshape(B, T, CONV_CH)[:, T - (CONV_W - 1):, :]

    if k_cache is None:
        tq = _pick_tile(T, (512, 256, 128))
        o_b = _attn_prompt(lam, qb, kbh, vbh, sw, tq, lam_init)
    else:
        o_b = _attn_sample(lam, qb.reshape(B, T, W_B), kbh.reshape(B, T, W_B), vbh.reshape(B, T, W_B),
                           k_cache.reshape(B, -1, W_B), v_cache.reshape(B, -1, W_B), sw, lam_init)
        o_b = o_b.reshape(n_tok, W_B)

    x1 = _outproj(x2d, o_a.reshape(n_tok, W_A), o_b, w_out, tm)

    tt = _pick_tile(n_tok, (256, 128))
    h2, i1, i2, gate = _peer_score(x1, n2, wq, kblk, tt)
    g = _peer_gate(i1, i2, gate, _pick_tile(n_tok, (128,)))
    y = _peer_expert(h2, ut, v, g, x1, nf, _pick_tile(n_tok, (1024, 512, 256, 128)), PEER_NA * N_KEYS)
    return (y.reshape(B, T, D_MODEL), kb.reshape(B, T, H_B, 2 * DH_B), vb.reshape(B, T, H_B, DV_B),
            conv_new, ssm_new)


def _prep_weights(norm1_w, w_in, conv_w, a_log, dt_bias, gdn_norm_w, subln_w, w_out,
                  norm2_w, peer_wq, peer_k1, peer_k2, peer_u, peer_v, norm_f_w):
    o_beta = CONV_CH + W_A
    w_main = jnp.concatenate([w_in[:, :o_beta], w_in[:, o_beta + 2 * H_A:]], axis=1).astype(BF16)
    w_gate = w_in[:, o_beta:o_beta + 2 * H_A]
    w_ba = jnp.pad(w_gate, ((0, 0), (0, LANES - 2 * H_A))).astype(BF16)
    w_bat = w_gate.T.astype(BF16)
    zeros4 = jnp.zeros((H_A,), F32)
    a8 = jnp.concatenate([zeros4, a_log.astype(F32)])
    d8 = jnp.concatenate([zeros4, dt_bias.astype(F32)])
    arow = jnp.pad(a8, (0, LANES - 2 * H_A)).reshape(1, LANES)
    drow = jnp.pad(d8, (0, LANES - 2 * H_A)).reshape(1, LANES)
    acol = a8.reshape(2 * H_A, 1)
    dcol = d8.reshape(2 * H_A, 1)
    halves = jnp.stack([peer_k1, peer_k2]).astype(F32)
    eye = jnp.eye(2 * PEER_HEADS, dtype=F32)
    kblk = jnp.einsum("gj,gnd->gnjd", eye, jnp.tile(halves, (PEER_HEADS, 1, 1)))
    kblk = kblk.reshape(2 * PEER_HEADS * N_KEYS, D_MODEL).astype(BF16)
    return (norm1_w.reshape(1, D_MODEL), w_main, w_ba, w_bat, conv_w, arow, drow, acol, dcol,
            gdn_norm_w.reshape(1, DK_A), subln_w.reshape(1, DV_B), w_out.astype(BF16),
            norm2_w.reshape(1, D_MODEL), peer_wq.astype(BF16), kblk,
            jnp.swapaxes(peer_u, 1, 2).astype(BF16), peer_v.astype(BF16), norm_f_w.reshape(1, D_MODEL))


def kernel(x_prompt, x_sample, cache_k, cache_v, state_conv, state_gdn, norm1_w, w_in, conv_w, a_log, dt_bias, gdn_norm_w, lam_q1, lam_k1, lam_q2, lam_k2, subln_w, w_out, norm2_w, peer_wq, peer_k1, peer_k2, peer_u, peer_v, norm_f_w):
    depth = w_in.shape[0]
    assert depth == 1, "the final norm is fused into the (single) layer's last kernel"
    l = 0
    lam_init = 0.8 - 0.6 * math.exp(-0.3 * l)
    lam = (jnp.exp(jnp.sum(lam_q1[l].astype(F32) * lam_k1[l].astype(F32)))
           - jnp.exp(jnp.sum(lam_q2[l].astype(F32) * lam_k2[l].astype(F32))) + lam_init).reshape(1)
    layer = lambda a: a.reshape(a.shape[1:])
    wts = _prep_weights(layer(norm1_w), layer(w_in), layer(conv_w), layer(a_log), layer(dt_bias),
                        layer(gdn_norm_w), layer(subln_w), layer(w_out), layer(norm2_w), layer(peer_wq),
                        layer(peer_k1), layer(peer_k2), peer_u, peer_v, norm_f_w)
    bp = x_prompt.shape[0]
    conv0 = jnp.zeros((bp, CONV_W - 1, CONV_CH), F32)
    ssm0 = jnp.zeros((bp, H_A, DK_A, DK_A), F32)
    yp, k1, v1, c1, s1 = _layer(x_prompt, conv0, ssm0, None, None, lam, lam_init, wts)
    ys, k2, v2, c2, s2 = _layer(x_sample, layer(state_conv), layer(state_gdn), layer(cache_k), layer(cache_v),
                                lam, lam_init, wts)
    st = lambda a: a[None]
    return (yp, ys, st(k1), st(v1), st(c1), st(s1), st(k2), st(v2), st(c2), st(s2))
```

```python
import functools
import math

import jax
import jax.numpy as jnp
from jax import lax
from jax.experimental import pallas as pl
from jax.experimental.pallas import tpu as pltpu

F32 = jnp.float32
BF16 = jnp.bfloat16
HIGHEST = lax.Precision.HIGHEST

D_MODEL = 1024
CHUNK = 64
H_A = 4
DK_A = 128
CONV_W = 4
W_A = H_A * DK_A
CONV_CH = 3 * W_A
H_B = 4
DH_B = 64
DV_B = 128
W_B = H_B * DV_B
DIFF_EPS = 1e-5
N_KEYS = 128
N_EXPERTS = N_KEYS * N_KEYS
PEER_HEADS = 8
PEER_TOPK = 16
PEER_DK_HALF = 64
PEER_NA = 8
EPS = 1e-6
LANES = 128
SUBLANES = 8
VMEM_LIMIT = 56 * 1024 * 1024

NT_DIMS = (((1,), (1,)), ((), ()))
TN_DIMS = (((0,), (0,)), ((), ()))


def _sigmoid(x):
    return 1.0 / (1.0 + jnp.exp(-x))


def _softplus(x):
    return jnp.maximum(x, 0.0) + jnp.log(1.0 + jnp.exp(-jnp.abs(x)))


def _dot_hi(a, b):
    return jnp.dot(a, b, precision=HIGHEST, preferred_element_type=F32)


def _dot_bf(a, b, dims=(((1,), (0,)), ((), ()))):
    return lax.dot_general(a.astype(BF16), b.astype(BF16), dims, preferred_element_type=F32)


def _split_bf(a):
    hi = a.astype(BF16)
    return hi, (a - hi.astype(F32)).astype(BF16)


def _dot_split(a, b):
    d = lambda x, y: jnp.dot(x, y, preferred_element_type=F32)
    return d(a[0], b[0]) + (d(a[0], b[1]) + d(a[1], b[0]))


def _inproj_kernel(x_ref, n1_ref, w_ref, wba_ref, wbat_ref,
                   conv_ref, z_ref, qb_ref, kb_ref, vb_ref, kbh_ref, vbh_ref, ba_ref, bat_ref):
    x = x_ref[...]
    ms = jnp.mean(x * x, axis=-1, keepdims=True)
    h = (x * lax.rsqrt(ms + EPS) * n1_ref[...]).astype(BF16)

    def mm(c0, c1):
        return jnp.dot(h, w_ref[:, c0:c1], preferred_element_type=F32)

    conv_ref[...] = mm(0, CONV_CH)
    z_ref[...] = mm(CONV_CH, CONV_CH + W_A)
    o = CONV_CH + W_A
    qb_ref[...] = (mm(o, o + W_B) * (DH_B ** -0.5)).astype(BF16)
    kb = mm(o + W_B, o + 2 * W_B)
    kb_ref[...] = kb
    kbh_ref[...] = kb.astype(BF16)
    vb = mm(o + 2 * W_B, o + 3 * W_B)
    vb_ref[...] = vb
    vbh_ref[...] = vb.astype(BF16)
    ba_ref[...] = jnp.dot(h, wba_ref[...], preferred_element_type=F32)
    bat_ref[...] = lax.dot_general(wbat_ref[...], h, NT_DIMS, preferred_element_type=F32)


def _inproj(x2d, n1, w_main, w_ba, w_bat, tm):
    T = x2d.shape[0]
    row = lambda c: pl.BlockSpec((tm, c), lambda i: (i, 0))
    full = lambda a: pl.BlockSpec(a.shape, lambda i: (0, 0))
    out_shape = (
        jax.ShapeDtypeStruct((T, CONV_CH), F32),
        jax.ShapeDtypeStruct((T, W_A), F32),
        jax.ShapeDtypeStruct((T, W_B), BF16),
        jax.ShapeDtypeStruct((T, W_B), F32),
        jax.ShapeDtypeStruct((T, W_B), F32),
        jax.ShapeDtypeStruct((T, W_B), BF16),
        jax.ShapeDtypeStruct((T, W_B), BF16),
        jax.ShapeDtypeStruct((T, LANES), F32),
        jax.ShapeDtypeStruct((SUBLANES, T), F32),
    )
    out_specs = (row(CONV_CH), row(W_A), row(W_B), row(W_B), row(W_B), row(W_B), row(W_B), row(LANES),
                 pl.BlockSpec((SUBLANES, tm), lambda i: (0, i)))
    return pl.pallas_call(
        _inproj_kernel,
        out_shape=out_shape,
        grid=(T // tm,),
        in_specs=[row(D_MODEL), full(n1), full(w_main), full(w_ba), full(w_bat)],
        out_specs=out_specs,
        compiler_params=pltpu.CompilerParams(dimension_semantics=("arbitrary",), vmem_limit_bytes=VMEM_LIMIT),
        name="inproj",
    )(x2d, n1, w_main, w_ba, w_bat)


def _gdn_kernel(conv_ref, z_ref, ba_ref, bat_ref, prev_ref, s0_ref, cw_ref, arow_ref, drow_ref,
                acol_ref, dcol_ref, nw_ref, o_ref, sout_ref, up_ref, s_ref, *, C, nc):
    t = pl.program_id(1)
    tt = nc * C

    @pl.when(t == 0)
    def _():
        up_ref[0:SUBLANES, :] = prev_ref[...]
        s_ref[...] = s0_ref[...]

    up_ref[SUBLANES:SUBLANES + tt, :] = conv_ref[...]
    y = up_ref[pl.ds(SUBLANES - CONV_W + 1, tt), :] * cw_ref[0:1, :]
    for j in range(1, CONV_W):
        y = y + up_ref[pl.ds(SUBLANES - CONV_W + 1 + j, tt), :] * cw_ref[j:j + 1, :]
    qkv = y * _sigmoid(y)
    tail = up_ref[tt:tt + SUBLANES, :]
    up_ref[0:SUBLANES, :] = tail

    ri = lax.broadcasted_iota(jnp.int32, (C, C), 0)
    ci = lax.broadcasted_iota(jnp.int32, (C, C), 1)
    incl = ri >= ci
    strict = ri > ci
    tril = incl.astype(F32)
    triu = (ri <= ci).astype(F32)
    eye = (ri == ci).astype(F32)

    ba = ba_ref[...]
    beta_all = _sigmoid(ba)
    g_all = -jnp.exp(arow_ref[...]) * _softplus(ba + drow_ref[...])
    nw = nw_ref[...]

    probs = []
    for c in range(nc):
        rows = slice(c * C, (c + 1) * C)
        gc_all = _dot_hi(tril, g_all[rows])
        g_t = -jnp.exp(acol_ref[...]) * _softplus(bat_ref[c] + dcol_ref[...])
        gc_t = _dot_hi(g_t, triu)
        for h in range(H_A):
            gcc = gc_all[:, H_A + h:H_A + h + 1]
            gcr = gc_t[H_A + h:H_A + h + 1, :]
            beta = beta_all[rows, h:h + 1]
            decay = jnp.where(incl, jnp.exp(jnp.where(incl, gcc - gcr, 0.0)), 0.0)
            qh = qkv[rows, h * DK_A:(h + 1) * DK_A]
            kh = qkv[rows, W_A + h * DK_A:W_A + (h + 1) * DK_A]
            vh = qkv[rows, 2 * W_A + h * DK_A:2 * W_A + (h + 1) * DK_A]
            qh = qh * lax.rsqrt(jnp.sum(qh * qh, axis=-1, keepdims=True) + EPS) * (DK_A ** -0.5)
            kh = kh * lax.rsqrt(jnp.sum(kh * kh, axis=-1, keepdims=True) + EPS)
            kb = kh * beta
            n = jnp.where(strict, -(_dot_bf(kb, kh, NT_DIMS) * decay), 0.0)
            egc = jnp.exp(gcc)
            g_last = gcc[C - 1:C, :]
            probs.append(dict(h=h, rows=rows, n=n, vb=(vh * beta).astype(BF16),
                              kbe=(kb * egc).astype(BF16), qg=(qh * egc).astype(BF16),
                              attn=(_dot_bf(qh, kh, NT_DIMS) * decay).astype(BF16),
                              kd=(kh * jnp.exp(g_last - gcc)).astype(BF16), eg=jnp.exp(g_last)))

    tinv = [eye + pr["n"] for pr in probs]
    pw = [_split_bf(pr["n"]) for pr in probs]
    for _ in range(int(math.log2(C)) - 1):
        pw = [_split_bf(_dot_split(p, p)) for p in pw]
        tinv = [ti + _dot_split(_split_bf(ti), p) for ti, p in zip(tinv, pw)]
    for pr, ti in zip(probs, tinv):
        ti = ti.astype(BF16)
        pr["u"] = jnp.dot(ti, pr["vb"], preferred_element_type=F32)
        pr["w"] = jnp.dot(ti, pr["kbe"], preferred_element_type=F32).astype(BF16)

    state = [s_ref[h] for h in range(H_A)]
    for pr in probs:
        h, rows = pr["h"], pr["rows"]
        s = state[h]
        s_bf = s.astype(BF16)
        v_new = (pr["u"] - jnp.dot(pr["w"], s_bf, preferred_element_type=F32)).astype(BF16)
        o = (jnp.dot(pr["qg"], s_bf, preferred_element_type=F32)
             + jnp.dot(pr["attn"], v_new, preferred_element_type=F32))
        state[h] = s * pr["eg"] + lax.dot_general(pr["kd"], v_new, TN_DIMS, preferred_element_type=F32)
        on = o * lax.rsqrt(jnp.mean(o * o, axis=-1, keepdims=True) + EPS) * nw
        zz = z_ref[rows, h * DK_A:(h + 1) * DK_A]
        o_ref[rows, h * DK_A:(h + 1) * DK_A] = (on * (zz * _sigmoid(zz))).astype(o_ref.dtype)
    for h in range(H_A):
        s_ref[h] = state[h]

    @pl.when(t == pl.num_programs(1) - 1)
    def _():
        sout_ref[...] = s_ref[...]


def _gdn(conv_in, z, ba, bat, prev8, s0, conv_w, arow, drow, acol, dcol, nw, C, nc):
    B, T, _ = conv_in.shape
    tt = nc * C
    tok = lambda c: pl.BlockSpec((None, tt, c), lambda b, t: (b, t, 0))
    full2 = lambda a: pl.BlockSpec(a.shape, lambda b, t: (0, 0))
    return pl.pallas_call(
        functools.partial(_gdn_kernel, C=C, nc=nc),
        out_shape=(jax.ShapeDtypeStruct((B, T, W_A), BF16),
                   jax.ShapeDtypeStruct((B, H_A, DK_A, DK_A), F32)),
        grid=(B, T // tt),
        in_specs=[tok(CONV_CH), tok(W_A), tok(LANES),
                  pl.BlockSpec((None, nc, SUBLANES, C), lambda b, t: (b, t, 0, 0)),
                  pl.BlockSpec((None, SUBLANES, CONV_CH), lambda b, t: (b, 0, 0)),
                  pl.BlockSpec((None, H_A, DK_A, DK_A), lambda b, t: (b, 0, 0, 0)),
                  full2(conv_w), full2(arow), full2(drow), full2(acol), full2(dcol), full2(nw)],
        out_specs=(tok(W_A), pl.BlockSpec((None, H_A, DK_A, DK_A), lambda b, t: (b, 0, 0, 0))),
        scratch_shapes=[pltpu.VMEM((SUBLANES + tt, CONV_CH), F32), pltpu.VMEM((H_A, DK_A, DK_A), F32)],
        compiler_params=pltpu.CompilerParams(dimension_semantics=("arbitrary", "arbitrary"),
                                             vmem_limit_bytes=VMEM_LIMIT),
        name="gdn",
    )(conv_in, z, ba, bat, prev8, s0, conv_w, arow, drow, acol, dcol, nw)


def _split_maps(q):
    lane = lax.broadcasted_iota(jnp.int32, q.shape, 1)
    zero = jnp.zeros_like(q)
    return jnp.concatenate([jnp.where(lane < DH_B, q, zero), jnp.where(lane >= DH_B, q, zero)], axis=0)


def _subln(acc, l, lam, sw, tq, lam_init):
    o = acc[:tq] / l[:tq] - lam * (acc[tq:] / l[tq:])
    return o * lax.rsqrt(jnp.mean(o * o, axis=-1, keepdims=True) + DIFF_EPS) * sw * (1.0 - lam_init)


def _attn_prompt_kernel(lam_ref, q_ref, k_ref, v_ref, sw_ref, o_ref, qs_ref, s0_ref, s1_ref, m_ref, l_ref,
                        acc_ref, *, tq, rc, lam_init):
    qi = pl.program_id(1)
    qs_ref[...] = _split_maps(q_ref[...])
    m_ref[...] = jnp.full(m_ref.shape, -jnp.inf, F32)
    l_ref[...] = jnp.zeros(l_ref.shape, F32)
    acc_ref[...] = jnp.zeros(acc_ref.shape, F32)
    chunks = [slice(c * rc, (c + 1) * rc) for c in range(2 * tq // rc)]
    rep = tq // LANES

    def scores(j, s_ref):
        kblk = k_ref[pl.ds(pl.multiple_of(j * tq, tq), tq), :]
        for rows in chunks:
            s_ref[rows, :] = lax.dot_general(qs_ref[rows, :], kblk, NT_DIMS, preferred_element_type=F32)

    def softmax_pv(j, s_ref, masked):
        vblk = v_ref[pl.ds(pl.multiple_of(j * tq, tq), tq), :]
        for rows in chunks:
            s = s_ref[rows, :]
            if masked:
                r = lax.broadcasted_iota(jnp.int32, s.shape, 0) + (rows.start % tq)
                c = lax.broadcasted_iota(jnp.int32, s.shape, 1)
                s = jnp.where((c // CHUNK) <= (r // CHUNK), s, -jnp.inf)
            m_old = m_ref[rows, :]
            m_new = jnp.maximum(m_old, jnp.max(s, axis=-1, keepdims=True))
            alpha = jnp.exp(m_old - m_new)
            p = jnp.exp(s - jnp.concatenate([m_new] * rep, axis=1))
            l_ref[rows, :] = alpha * l_ref[rows, :] + jnp.sum(p, axis=-1, keepdims=True)
            acc_ref[rows, :] = alpha * acc_ref[rows, :] + jnp.dot(p.astype(BF16), vblk,
                                                                  preferred_element_type=F32)
            m_ref[rows, :] = m_new

    scores(qi, s0_ref)
    softmax_pv(qi, s0_ref, True)

    @pl.when(qi > 0)
    def _():
        scores(0, s0_ref)

        def pair(i, carry):
            j = 2 * i
            scores(j + 1, s1_ref)
            softmax_pv(j, s0_ref, False)
            scores(j + 2, s0_ref)
            softmax_pv(j + 1, s1_ref, False)
            return carry

        lax.fori_loop(0, qi // 2, pair, 0)

        @pl.when(qi % 2 == 1)
        def _():
            softmax_pv(qi - 1, s0_ref, False)

    o_ref[...] = _subln(acc_ref[...], l_ref[...], lam_ref[0], sw_ref[...], tq, lam_init).astype(o_ref.dtype)


def _attn_prompt(lam, q, k, v, sw, tq, lam_init):
    T = q.shape[0]
    return pl.pallas_call(
        functools.partial(_attn_prompt_kernel, tq=tq, rc=min(512, tq), lam_init=lam_init),
        out_shape=jax.ShapeDtypeStruct((T, W_B), BF16),
        grid=(H_B, T // tq),
        in_specs=[pl.BlockSpec(memory_space=pltpu.SMEM),
                  pl.BlockSpec((tq, DV_B), lambda h, i: (i, h)),
                  pl.BlockSpec((T, DV_B), lambda h, i: (0, h)),
                  pl.BlockSpec((T, DV_B), lambda h, i: (0, h)),
                  pl.BlockSpec((1, DV_B), lambda h, i: (0, 0))],
        out_specs=pl.BlockSpec((tq, DV_B), lambda h, i: (i, h)),
        scratch_shapes=[pltpu.VMEM((2 * tq, DV_B), BF16), pltpu.VMEM((2 * tq, tq), F32),
                        pltpu.VMEM((2 * tq, tq), F32),
                        pltpu.VMEM((2 * tq, LANES), F32), pltpu.VMEM((2 * tq, LANES), F32),
                        pltpu.VMEM((2 * tq, DV_B), F32)],
        compiler_params=pltpu.CompilerParams(dimension_semantics=("arbitrary", "arbitrary"),
                                             vmem_limit_bytes=VMEM_LIMIT),
        name="attn_prompt",
    )(lam, q, k, v, sw)


def _attn_sample_kernel(lam_ref, q_ref, kn_ref, vn_ref, ck_ref, cv_ref, sw_ref, o_ref, *, tq, lam_init):
    for h in range(H_B):
        cols = slice(h * DV_B, (h + 1) * DV_B)
        qs = _split_maps(q_ref[:, cols])
        kc = ck_ref[:, h, :].astype(BF16)
        vc = cv_ref[:, h, :].astype(BF16)
        sc = lax.dot_general(qs, kc, NT_DIMS, preferred_element_type=F32)
        sn = lax.dot_general(qs, kn_ref[:, cols], NT_DIMS, preferred_element_type=F32)
        m = jnp.maximum(jnp.max(sc, axis=-1, keepdims=True), jnp.max(sn, axis=-1, keepdims=True))
        pc = jnp.exp(sc - m)
        pn = jnp.exp(sn - m)
        l = jnp.sum(pc, axis=-1, keepdims=True) + jnp.sum(pn, axis=-1, keepdims=True)
        acc = (jnp.dot(pc.astype(BF16), vc, preferred_element_type=F32)
               + jnp.dot(pn.astype(BF16), vn_ref[:, cols], preferred_element_type=F32))
        o_ref[:, cols] = _subln(acc, l, lam_ref[0], sw_ref[...], tq, lam_init).astype(o_ref.dtype)


def _attn_sample(lam, q, kn, vn, ck, cv, sw, lam_init):
    B, tq, _ = q.shape
    P = ck.shape[1]
    new = pl.BlockSpec((None, tq, W_B), lambda b: (b, 0, 0))
    cache = pl.BlockSpec((None, P, H_B, DV_B), lambda b: (b, 0, 0, 0))
    return pl.pallas_call(
        functools.partial(_attn_sample_kernel, tq=tq, lam_init=lam_init),
        out_shape=jax.ShapeDtypeStruct((B, tq, W_B), BF16),
        grid=(B,),
        in_specs=[pl.BlockSpec(memory_space=pltpu.SMEM), new, new, new, cache, cache,
                  pl.BlockSpec((1, DV_B), lambda b: (0, 0))],
        out_specs=new,
        compiler_params=pltpu.CompilerParams(dimension_semantics=("arbitrary",), vmem_limit_bytes=VMEM_LIMIT),
        name="attn_sample",
    )(lam, q, kn, vn, ck, cv, sw)


def _outproj_kernel(x_ref, oa_ref, ob_ref, w_ref, y_ref):
    y_ref[...] = (x_ref[...]
                  + jnp.dot(oa_ref[...], w_ref[0:W_A, :], preferred_element_type=F32)
                  + jnp.dot(ob_ref[...], w_ref[W_A:W_A + W_B, :], preferred_element_type=F32))


def _outproj(x2d, oa, ob, w_out, tm):
    T = x2d.shape[0]
    row = lambda c: pl.BlockSpec((tm, c), lambda i: (i, 0))
    return pl.pallas_call(
        _outproj_kernel,
        out_shape=jax.ShapeDtypeStruct((T, D_MODEL), F32),
        grid=(T // tm,),
        in_specs=[row(D_MODEL), row(W_A), row(W_B), pl.BlockSpec(w_out.shape, lambda i: (0, 0))],
        out_specs=row(D_MODEL),
        compiler_params=pltpu.CompilerParams(dimension_semantics=("arbitrary",), vmem_limit_bytes=VMEM_LIMIT),
        name="outproj",
    )(x2d, oa, ob, w_out)


_NEG_INF = float("-inf")
_PAIR_SLAB = SUBLANES
PEER_SLOTS = PEER_HEADS * PEER_TOPK


def _top16(s):
    n = s.shape[0]
    iota = lax.broadcasted_iota(jnp.int32, s.shape, 0)
    vals, idxs = [], []
    for _ in range(PEER_TOPK):
        m = jnp.max(s, axis=0, keepdims=True)
        first = jnp.min(jnp.where(s == m, iota, n), axis=0, keepdims=True)
        s = jnp.where(iota == first, _NEG_INF, s)
        vals.append(m)
        idxs.append(first.astype(F32))
    return vals, idxs


def _top16_pairs(v1, v2):
    t = v1[0].shape[1]
    v1m = jnp.concatenate(v1, axis=0)
    v2m = jnp.concatenate(v2, axis=0)
    v2s = v2m[0:_PAIR_SLAB]
    i16 = lax.broadcasted_iota(jnp.int32, (PEER_TOPK, t), 0)
    i8 = lax.broadcasted_iota(jnp.int32, (_PAIR_SLAB, t), 0)
    slabs = [v1[0] + v2m, v1[1] + v2s]
    flats = [i16, PEER_TOPK + i8]
    for p in range(2, _PAIR_SLAB):
        slabs.append(jnp.where(i8 < PEER_TOPK // (p + 1), v1[p] + v2s, _NEG_INF))
        flats.append(PEER_TOPK * p + i8)
    slabs.append(v1m[_PAIR_SLAB:] + v2[0])
    flats.append(PEER_TOPK * (_PAIR_SLAB + i8))
    c = jnp.concatenate(slabs, axis=0)
    flat = jnp.concatenate(flats, axis=0)
    big = PEER_TOPK * PEER_TOPK
    sums, picks = [], []
    for _ in range(PEER_TOPK):
        m = jnp.max(c, axis=0, keepdims=True)
        first = jnp.min(jnp.where(c == m, flat, big), axis=0, keepdims=True)
        c = jnp.where(flat == first, _NEG_INF, c)
        sums.append(m)
        picks.append(first)
    pick = jnp.concatenate(picks, axis=0)
    p_sel = jnp.right_shift(pick, 4).astype(F32)
    q_sel = jnp.bitwise_and(pick, PEER_TOPK - 1).astype(F32)
    return jnp.concatenate(sums, axis=0), p_sel, q_sel


def _lookup(ranks, rows):
    out = jnp.zeros(ranks.shape, F32)
    for p in range(PEER_TOPK):
        out = jnp.where(ranks == float(p), rows[p], out)
    return out


def _peer_score_kernel(x_ref, n2_ref, wq_ref, kblk_ref, h_ref, i1_ref, i2_ref, w_ref,
                       st_ref, a_ref, b_ref, g_ref, *, tt):
    x = x_ref[...]
    ms = jnp.mean(x * x, axis=-1, keepdims=True)
    h = (x * lax.rsqrt(ms + EPS) * n2_ref[...]).astype(BF16)
    h_ref[...] = h
    q = jnp.dot(h, wq_ref[...], preferred_element_type=F32)
    st_ref[...] = lax.dot_general(kblk_ref[...], q.astype(BF16), NT_DIMS, preferred_element_type=F32)

    def head(hd, carry):
        r1 = pl.multiple_of(hd * (2 * N_KEYS), 2 * N_KEYS)
        slot = pl.multiple_of(hd * PEER_TOPK, PEER_TOPK)
        for g in range(tt // LANES):
            cols = slice(g * LANES, (g + 1) * LANES)
            v1, i1 = _top16(st_ref[pl.ds(r1, N_KEYS), cols])
            v2, i2 = _top16(st_ref[pl.ds(r1 + N_KEYS, N_KEYS), cols])
            sums, p_sel, q_sel = _top16_pairs(v1, v2)
            e = jnp.exp(sums - sums[0:1])
            a_ref[pl.ds(slot, PEER_TOPK), cols] = _lookup(p_sel, i1)
            b_ref[pl.ds(slot, PEER_TOPK), cols] = _lookup(q_sel, i2)
            g_ref[pl.ds(slot, PEER_TOPK), cols] = e / jnp.sum(e, axis=0, keepdims=True)
        return carry

    lax.fori_loop(0, PEER_HEADS, head, 0)
    i1_ref[...] = a_ref[...].T
    i2_ref[...] = b_ref[...].T
    w_ref[...] = g_ref[...].T


def _peer_score(x1, n2, wq, kblk, tt):
    T = x1.shape[0]
    slots = jax.ShapeDtypeStruct((T, PEER_SLOTS), F32)
    sspec = pl.BlockSpec((tt, PEER_SLOTS), lambda i: (i, 0))
    return pl.pallas_call(
        functools.partial(_peer_score_kernel, tt=tt),
        out_shape=(jax.ShapeDtypeStruct((T, D_MODEL), BF16), slots, slots, slots),
        grid=(T // tt,),
        in_specs=[pl.BlockSpec((tt, D_MODEL), lambda i: (i, 0)),
                  pl.BlockSpec(n2.shape, lambda i: (0, 0)),
                  pl.BlockSpec(wq.shape, lambda i: (0, 0)),
                  pl.BlockSpec(kblk.shape, lambda i: (0, 0))],
        out_specs=(pl.BlockSpec((tt, D_MODEL), lambda i: (i, 0)), sspec, sspec, sspec),
        scratch_shapes=[pltpu.VMEM((2 * PEER_HEADS * N_KEYS, tt), F32)]
                       + [pltpu.VMEM((PEER_SLOTS, tt), F32)] * 3,
        compiler_params=pltpu.CompilerParams(dimension_semantics=("arbitrary",), vmem_limit_bytes=VMEM_LIMIT),
        name="peer_score",
    )(x1, n2, wq, kblk)


_GATE_GROUP = 2 * SUBLANES


def _peer_gate_kernel(i1_ref, i2_ref, w_ref, g_ref, *, tg):
    key = lax.broadcasted_iota(jnp.int32, (N_KEYS, PEER_SLOTS), 0).astype(F32)

    def group(gi, carry):
        t0 = pl.multiple_of(gi * _GATE_GROUP, _GATE_GROUP)
        per_token = []
        for s in range(_GATE_GROUP):
            i1 = i1_ref[pl.ds(t0 + s, 1), :]
            i2 = i2_ref[pl.ds(t0 + s, 1), :]
            w = w_ref[pl.ds(t0 + s, 1), :]
            lhs = jnp.where(key == i1, w, 0.0).astype(BF16)
            rhs = jnp.where(key == i2, 1.0, 0.0).astype(BF16)
            per_token.append(lax.dot_general(lhs, rhs, NT_DIMS, preferred_element_type=F32))
        by_key = jnp.swapaxes(jnp.stack(per_token, axis=0), 0, 1)
        for a in range(N_KEYS):
            g_ref[pl.ds(t0, _GATE_GROUP), a * N_KEYS:(a + 1) * N_KEYS] = by_key[a].astype(BF16)
        return carry

    lax.fori_loop(0, tg // _GATE_GROUP, group, 0)


def _peer_gate(i1, i2, w, tg):
    T = i1.shape[0]
    sspec = pl.BlockSpec((tg, PEER_SLOTS), lambda i: (i, 0))
    return pl.pallas_call(
        functools.partial(_peer_gate_kernel, tg=tg),
        out_shape=jax.ShapeDtypeStruct((T, N_EXPERTS), BF16),
        grid=(T // tg,),
        in_specs=[sspec, sspec, sspec],
        out_specs=pl.BlockSpec((tg, N_EXPERTS), lambda i: (i, 0)),
        compiler_params=pltpu.CompilerParams(dimension_semantics=("arbitrary",), vmem_limit_bytes=VMEM_LIMIT),
        name="peer_gate",
    )(i1, i2, w)


def _peer_expert_kernel(h_ref, ut_ref, v_ref, g_ref, x_ref, nf_ref, y_ref, acc_ref):
    e = pl.program_id(1)

    @pl.when(e == 0)
    def _():
        acc_ref[...] = jnp.zeros(acc_ref.shape, F32)

    x = jnp.dot(h_ref[...], ut_ref[...], preferred_element_type=F32)
    act = 0.5 * x * (1.0 + lax.erf(x * (2.0 ** -0.5)))
    ga = (g_ref[...].astype(F32) * act).astype(BF16)
    acc_ref[...] += jnp.dot(ga, v_ref[...], preferred_element_type=F32)

    @pl.when(e == pl.num_programs(1) - 1)
    def _():
        xo = x_ref[...] + acc_ref[...]
        ms = jnp.mean(xo * xo, axis=-1, keepdims=True)
        y_ref[...] = xo * lax.rsqrt(ms + EPS) * nf_ref[...]


def _peer_expert(h, ut, v, g, x1, nf, tt, et):
    T = x1.shape[0]
    return pl.pallas_call(
        _peer_expert_kernel,
        out_shape=jax.ShapeDtypeStruct((T, D_MODEL), F32),
        grid=(T // tt, N_EXPERTS // et),
        in_specs=[pl.BlockSpec((tt, D_MODEL), lambda i, e: (i, 0)),
                  pl.BlockSpec((None, D_MODEL, et), lambda i, e: (0, 0, e)),
                  pl.BlockSpec((None, et, D_MODEL), lambda i, e: (0, e, 0)),
                  pl.BlockSpec((tt, et), lambda i, e: (i, e)),
                  pl.BlockSpec((tt, D_MODEL), lambda i, e: (i, 0), pipeline_mode=pl.Buffered(1)),
                  pl.BlockSpec(nf.shape, lambda i, e: (0, 0))],
        out_specs=pl.BlockSpec((tt, D_MODEL), lambda i, e: (i, 0)),
        scratch_shapes=[pltpu.VMEM((tt, D_MODEL), F32)],
        compiler_params=pltpu.CompilerParams(dimension_semantics=("arbitrary", "arbitrary"),
                                             vmem_limit_bytes=VMEM_LIMIT),
        name="peer_expert",
    )(h, ut, v, g, x1, nf)


def _pick_tile(T, prefs):
    for t in prefs:
        if T % t == 0:
            return t
    raise ValueError(f"no tile for {T}")


def _layer(x, conv_prev, ssm_prev, k_cache, v_cache, lam, lam_init, wts):
    (n1, w_main, w_ba, w_bat, conv_w, arow, drow, acol, dcol, gnw, sw, w_out, n2, wq, kblk, ut, v, nf) = wts
    B, T, _ = x.shape
    n_tok = B * T
    x2d = x.reshape(n_tok, D_MODEL)
    tm = _pick_tile(n_tok, (256, 128))
    conv_in, z, qb, kb, vb, kbh, vbh, ba, bat = _inproj(x2d, n1, w_main, w_ba, w_bat, tm)

    C = min(CHUNK, T)
    prev8 = jnp.pad(conv_prev.astype(F32), ((0, 0), (SUBLANES - CONV_W + 1, 0), (0, 0)))
    bat4 = bat.reshape(SUBLANES, B, T // C, C).transpose(1, 2, 0, 3)
    o_a, ssm_new = _gdn(conv_in.reshape(B, T, CONV_CH), z.reshape(B, T, W_A), ba.reshape(B, T, LANES), bat4,
                        prev8, ssm_prev.astype(F32), conv_w, arow, drow, acol, dcol, gnw, C,
                        _pick_tile(T // C, (4, 2, 1)))
    conv_new = conv_in.reshape(B, T, CONV_CH)[:, T - (CONV_W - 1):, :]

    if k_cache is None:
        tq = _pick_tile(T, (1024, 512, 256, 128))
        o_b = _attn_prompt(lam, qb, kbh, vbh, sw, tq, lam_init)
    else:
        o_b = _attn_sample(lam, qb.reshape(B, T, W_B), kbh.reshape(B, T, W_B), vbh.reshape(B, T, W_B),
                           k_cache, v_cache, sw, lam_init)
        o_b = o_b.reshape(n_tok, W_B)

    x1 = _outproj(x2d, o_a.reshape(n_tok, W_A), o_b, w_out, tm)

    tt = _pick_tile(n_tok, (256, 128))
    h2, i1, i2, gate = _peer_score(x1, n2, wq, kblk, tt)
    g = _peer_gate(i1, i2, gate, _pick_tile(n_tok, (128,)))
    y = _peer_expert(h2, ut, v, g, x1, nf, _pick_tile(n_tok, (1024, 512, 256, 128)), PEER_NA * N_KEYS)
    return (y.reshape(B, T, D_MODEL), kb.reshape(B, T, H_B, 2 * DH_B), vb.reshape(B, T, H_B, DV_B),
            conv_new, ssm_new)


def _prep_weights(norm1_w, w_in, conv_w, a_log, dt_bias, gdn_norm_w, subln_w, w_out,
                  norm2_w, peer_wq, peer_k1, peer_k2, peer_u, peer_v, norm_f_w):
    o_beta = CONV_CH + W_A
    w_main = jnp.concatenate([w_in[:, :o_beta], w_in[:, o_beta + 2 * H_A:]], axis=1).astype(BF16)
    w_gate = w_in[:, o_beta:o_beta + 2 * H_A]
    w_ba = jnp.pad(w_gate, ((0, 0), (0, LANES - 2 * H_A))).astype(BF16)
    w_bat = w_gate.T.astype(BF16)
    zeros4 = jnp.zeros((H_A,), F32)
    a8 = jnp.concatenate([zeros4, a_log.astype(F32)])
    d8 = jnp.concatenate([zeros4, dt_bias.astype(F32)])
    arow = jnp.pad(a8, (0, LANES - 2 * H_A)).reshape(1, LANES)
    drow = jnp.pad(d8, (0, LANES - 2 * H_A)).reshape(1, LANES)
    acol = a8.reshape(2 * H_A, 1)
    dcol = d8.reshape(2 * H_A, 1)
    halves = jnp.stack([peer_k1, peer_k2]).astype(F32)
    eye = jnp.eye(2 * PEER_HEADS, dtype=F32)
    kblk = jnp.einsum("gj,gnd->gnjd", eye, jnp.tile(halves, (PEER_HEADS, 1, 1)))
    kblk = kblk.reshape(2 * PEER_HEADS * N_KEYS, D_MODEL).astype(BF16)
    return (norm1_w.reshape(1, D_MODEL), w_main, w_ba, w_bat, conv_w, arow, drow, acol, dcol,
            gdn_norm_w.reshape(1, DK_A), subln_w.reshape(1, DV_B), w_out.astype(BF16),
            norm2_w.reshape(1, D_MODEL), peer_wq.astype(BF16), kblk,
            jnp.swapaxes(peer_u, 1, 2).astype(BF16), peer_v.astype(BF16), norm_f_w.reshape(1, D_MODEL))


def kernel(x_prompt, x_sample, cache_k, cache_v, state_conv, state_gdn, norm1_w, w_in, conv_w, a_log, dt_bias, gdn_norm_w, lam_q1, lam_k1, lam_q2, lam_k2, subln_w, w_out, norm2_w, peer_wq, peer_k1, peer_k2, peer_u, peer_v, norm_f_w):
    depth = w_in.shape[0]
    assert depth == 1, "the final norm is fused into the (single) layer's last kernel"
    l = 0
    lam_init = 0.8 - 0.6 * math.exp(-0.3 * l)
    lam = (jnp.exp(jnp.sum(lam_q1[l].astype(F32) * lam_k1[l].astype(F32)))
           - jnp.exp(jnp.sum(lam_q2[l].astype(F32) * lam_k2[l].astype(F32))) + lam_init).reshape(1)
    layer = lambda a: a.reshape(a.shape[1:])
    wts = _prep_weights(layer(norm1_w), layer(w_in), layer(conv_w), layer(a_log), layer(dt_bias),
                        layer(gdn_norm_w), layer(subln_w), layer(w_out), layer(norm2_w), layer(peer_wq),
                        layer(peer_k1), layer(peer_k2), peer_u, peer_v, norm_f_w)
    bp = x_prompt.shape[0]
    conv0 = jnp.zeros((bp, CONV_W - 1, CONV_CH), F32)
    ssm0 = jnp.zeros((bp, H_A, DK_A, DK_A), F32)
    yp, k1, v1, c1, s1 = _layer(x_prompt, conv0, ssm0, None, None, lam, lam_init, wts)
    ys, k2, v2, c2, s2 = _layer(x_sample, layer(state_conv), layer(state_gdn), layer(cache_k), layer(cache_v),
                                lam, lam_init, wts)
    st = lambda a: a[None]
    return (yp, ys, st(k1), st(v1), st(c1), st(s1), st(k2), st(v2), st(c2), st(s2))
```

```python
import functools
import math

import jax
import jax.numpy as jnp
from jax import lax
from jax.experimental import pallas as pl
from jax.experimental.pallas import tpu as pltpu

F32 = jnp.float32
BF16 = jnp.bfloat16
HIGHEST = lax.Precision.HIGHEST

D_MODEL = 1024
CHUNK = 64
H_A = 4
DK_A = 128
CONV_W = 4
W_A = H_A * DK_A
CONV_CH = 3 * W_A
H_B = 4
DH_B = 64
DV_B = 128
W_B = H_B * DV_B
DIFF_EPS = 1e-5
N_KEYS = 128
N_EXPERTS = N_KEYS * N_KEYS
PEER_HEADS = 8
PEER_TOPK = 16
PEER_DK_HALF = 64
PEER_NA = 16
EPS = 1e-6
LANES = 128
SUBLANES = 8
VMEM_LIMIT = 56 * 1024 * 1024

NT_DIMS = (((1,), (1,)), ((), ()))
TN_DIMS = (((0,), (0,)), ((), ()))


def _sigmoid(x):
    return 1.0 / (1.0 + jnp.exp(-x))


def _softplus(x):
    return jnp.maximum(x, 0.0) + jnp.log(1.0 + jnp.exp(-jnp.abs(x)))


def _dot_hi(a, b):
    return jnp.dot(a, b, precision=HIGHEST, preferred_element_type=F32)


def _dot_bf(a, b, dims=(((1,), (0,)), ((), ()))):
    return lax.dot_general(a.astype(BF16), b.astype(BF16), dims, preferred_element_type=F32)


def _split_bf(a):
    hi = a.astype(BF16)
    return hi, (a - hi.astype(F32)).astype(BF16)


def _dot_split(a, b):
    d = lambda x, y: jnp.dot(x, y, preferred_element_type=F32)
    return d(a[0], b[0]) + (d(a[0], b[1]) + d(a[1], b[0]))


def _inproj_kernel(x_ref, n1_ref, w_ref, wba_ref, wbat_ref,
                   conv_ref, z_ref, qb_ref, kb_ref, vb_ref, kbh_ref, vbh_ref, ba_ref, bat_ref):
    x = x_ref[...]
    ms = jnp.mean(x * x, axis=-1, keepdims=True)
    h = (x * lax.rsqrt(ms + EPS) * n1_ref[...]).astype(BF16)

    def mm(c0, c1):
        return jnp.dot(h, w_ref[:, c0:c1], preferred_element_type=F32)

    conv_ref[...] = mm(0, CONV_CH)
    z_ref[...] = mm(CONV_CH, CONV_CH + W_A)
    o = CONV_CH + W_A
    qb_ref[...] = (mm(o, o + W_B) * (DH_B ** -0.5)).astype(BF16)
    kb = mm(o + W_B, o + 2 * W_B)
    kb_ref[...] = kb
    kbh_ref[...] = kb.astype(BF16)
    vb = mm(o + 2 * W_B, o + 3 * W_B)
    vb_ref[...] = vb
    vbh_ref[...] = vb.astype(BF16)
    ba_ref[...] = jnp.dot(h, wba_ref[...], preferred_element_type=F32)
    bat_ref[...] = lax.dot_general(wbat_ref[...], h, NT_DIMS, preferred_element_type=F32)


def _inproj(x2d, n1, w_main, w_ba, w_bat, tm):
    T = x2d.shape[0]
    row = lambda c: pl.BlockSpec((tm, c), lambda i: (i, 0))
    full = lambda a: pl.BlockSpec(a.shape, lambda i: (0, 0))
    out_shape = (
        jax.ShapeDtypeStruct((T, CONV_CH), F32),
        jax.ShapeDtypeStruct((T, W_A), F32),
        jax.ShapeDtypeStruct((T, W_B), BF16),
        jax.ShapeDtypeStruct((T, W_B), F32),
        jax.ShapeDtypeStruct((T, W_B), F32),
        jax.ShapeDtypeStruct((T, W_B), BF16),
        jax.ShapeDtypeStruct((T, W_B), BF16),
        jax.ShapeDtypeStruct((T, LANES), F32),
        jax.ShapeDtypeStruct((SUBLANES, T), F32),
    )
    out_specs = (row(CONV_CH), row(W_A), row(W_B), row(W_B), row(W_B), row(W_B), row(W_B), row(LANES),
                 pl.BlockSpec((SUBLANES, tm), lambda i: (0, i)))
    return pl.pallas_call(
        _inproj_kernel,
        out_shape=out_shape,
        grid=(T // tm,),
        in_specs=[row(D_MODEL), full(n1), full(w_main), full(w_ba), full(w_bat)],
        out_specs=out_specs,
        compiler_params=pltpu.CompilerParams(dimension_semantics=("arbitrary",), vmem_limit_bytes=VMEM_LIMIT),
        name="inproj",
    )(x2d, n1, w_main, w_ba, w_bat)


def _gdn_kernel(conv_ref, z_ref, ba_ref, bat_ref, prev_ref, s0_ref, cw_ref, arow_ref, drow_ref,
                acol_ref, dcol_ref, nw_ref, o_ref, sout_ref, up_ref, s_ref, *, C, nc):
    t = pl.program_id(1)
    tt = nc * C

    @pl.when(t == 0)
    def _():
        up_ref[0:SUBLANES, :] = prev_ref[...]
        s_ref[...] = s0_ref[...]

    up_ref[SUBLANES:SUBLANES + tt, :] = conv_ref[...]
    y = up_ref[pl.ds(SUBLANES - CONV_W + 1, tt), :] * cw_ref[0:1, :]
    for j in range(1, CONV_W):
        y = y + up_ref[pl.ds(SUBLANES - CONV_W + 1 + j, tt), :] * cw_ref[j:j + 1, :]
    qkv = y * _sigmoid(y)
    tail = up_ref[tt:tt + SUBLANES, :]
    up_ref[0:SUBLANES, :] = tail

    ri = lax.broadcasted_iota(jnp.int32, (C, C), 0)
    ci = lax.broadcasted_iota(jnp.int32, (C, C), 1)
    incl = ri >= ci
    strict = ri > ci
    tril = incl.astype(F32)
    triu = (ri <= ci).astype(F32)
    eye = (ri == ci).astype(F32)

    ba = ba_ref[...]
    beta_all = _sigmoid(ba)
    g_all = -jnp.exp(arow_ref[...]) * _softplus(ba + drow_ref[...])
    nw = nw_ref[...]

    probs = []
    for c in range(nc):
        rows = slice(c * C, (c + 1) * C)
        gc_all = _dot_hi(tril, g_all[rows])
        g_t = -jnp.exp(acol_ref[...]) * _softplus(bat_ref[c] + dcol_ref[...])
        gc_t = _dot_hi(g_t, triu)
        for h in range(H_A):
            gcc = gc_all[:, H_A + h:H_A + h + 1]
            gcr = gc_t[H_A + h:H_A + h + 1, :]
            beta = beta_all[rows, h:h + 1]
            decay = jnp.where(incl, jnp.exp(jnp.where(incl, gcc - gcr, 0.0)), 0.0)
            qh = qkv[rows, h * DK_A:(h + 1) * DK_A]
            kh = qkv[rows, W_A + h * DK_A:W_A + (h + 1) * DK_A]
            vh = qkv[rows, 2 * W_A + h * DK_A:2 * W_A + (h + 1) * DK_A]
            qh = qh * lax.rsqrt(jnp.sum(qh * qh, axis=-1, keepdims=True) + EPS) * (DK_A ** -0.5)
            kh = kh * lax.rsqrt(jnp.sum(kh * kh, axis=-1, keepdims=True) + EPS)
            kb = kh * beta
            n = jnp.where(strict, -(_dot_bf(kb, kh, NT_DIMS) * decay), 0.0)
            egc = jnp.exp(gcc)
            g_last = gcc[C - 1:C, :]
            probs.append(dict(h=h, rows=rows, n=n, vb=(vh * beta).astype(BF16),
                              kbe=(kb * egc).astype(BF16), qg=(qh * egc).astype(BF16),
                              attn=(_dot_bf(qh, kh, NT_DIMS) * decay).astype(BF16),
                              kd=(kh * jnp.exp(g_last - gcc)).astype(BF16), eg=jnp.exp(g_last)))

    tinv = [eye + pr["n"] for pr in probs]
    pw = [_split_bf(pr["n"]) for pr in probs]
    for _ in range(int(math.log2(C)) - 1):
        pw = [_split_bf(_dot_split(p, p)) for p in pw]
        tinv = [ti + _dot_split(_split_bf(ti), p) for ti, p in zip(tinv, pw)]
    for pr, ti in zip(probs, tinv):
        ti = ti.astype(BF16)
        pr["u"] = jnp.dot(ti, pr["vb"], preferred_element_type=F32)
        pr["w"] = jnp.dot(ti, pr["kbe"], preferred_element_type=F32).astype(BF16)

    state = [s_ref[h] for h in range(H_A)]
    for pr in probs:
        h, rows = pr["h"], pr["rows"]
        s = state[h]
        s_bf = s.astype(BF16)
        v_new = (pr["u"] - jnp.dot(pr["w"], s_bf, preferred_element_type=F32)).astype(BF16)
        o = (jnp.dot(pr["qg"], s_bf, preferred_element_type=F32)
             + jnp.dot(pr["attn"], v_new, preferred_element_type=F32))
        state[h] = s * pr["eg"] + lax.dot_general(pr["kd"], v_new, TN_DIMS, preferred_element_type=F32)
        on = o * lax.rsqrt(jnp.mean(o * o, axis=-1, keepdims=True) + EPS) * nw
        zz = z_ref[rows, h * DK_A:(h + 1) * DK_A]
        o_ref[rows, h * DK_A:(h + 1) * DK_A] = (on * (zz * _sigmoid(zz))).astype(o_ref.dtype)
    for h in range(H_A):
        s_ref[h] = state[h]

    @pl.when(t == pl.num_programs(1) - 1)
    def _():
        sout_ref[...] = s_ref[...]


def _gdn(conv_in, z, ba, bat, prev8, s0, conv_w, arow, drow, acol, dcol, nw, C, nc):
    B, T, _ = conv_in.shape
    tt = nc * C
    tok = lambda c: pl.BlockSpec((None, tt, c), lambda b, t: (b, t, 0))
    full2 = lambda a: pl.BlockSpec(a.shape, lambda b, t: (0, 0))
    return pl.pallas_call(
        functools.partial(_gdn_kernel, C=C, nc=nc),
        out_shape=(jax.ShapeDtypeStruct((B, T, W_A), BF16),
                   jax.ShapeDtypeStruct((B, H_A, DK_A, DK_A), F32)),
        grid=(B, T // tt),
        in_specs=[tok(CONV_CH), tok(W_A), tok(LANES),
                  pl.BlockSpec((None, nc, SUBLANES, C), lambda b, t: (b, t, 0, 0)),
                  pl.BlockSpec((None, SUBLANES, CONV_CH), lambda b, t: (b, 0, 0)),
                  pl.BlockSpec((None, H_A, DK_A, DK_A), lambda b, t: (b, 0, 0, 0)),
                  full2(conv_w), full2(arow), full2(drow), full2(acol), full2(dcol), full2(nw)],
        out_specs=(tok(W_A), pl.BlockSpec((None, H_A, DK_A, DK_A), lambda b, t: (b, 0, 0, 0))),
        scratch_shapes=[pltpu.VMEM((SUBLANES + tt, CONV_CH), F32), pltpu.VMEM((H_A, DK_A, DK_A), F32)],
        compiler_params=pltpu.CompilerParams(dimension_semantics=("arbitrary", "arbitrary"),
                                             vmem_limit_bytes=VMEM_LIMIT),
        name="gdn",
    )(conv_in, z, ba, bat, prev8, s0, conv_w, arow, drow, acol, dcol, nw)


def _split_maps(q):
    lane = lax.broadcasted_iota(jnp.int32, q.shape, 1)
    zero = jnp.zeros_like(q)
    return jnp.concatenate([jnp.where(lane < DH_B, q, zero), jnp.where(lane >= DH_B, q, zero)], axis=0)


def _subln(acc, l, lam, sw, tq, lam_init):
    o = acc[:tq] / l[:tq] - lam * (acc[tq:] / l[tq:])
    return o * lax.rsqrt(jnp.mean(o * o, axis=-1, keepdims=True) + DIFF_EPS) * sw * (1.0 - lam_init)


def _attn_prompt_kernel(lam_ref, q_ref, k_ref, v_ref, sw_ref, o_ref, qs_ref, s0_ref, s1_ref, m_ref,
                        acc_ref, *, tq, rc, lam_init):
    qi = pl.program_id(1)
    qs_ref[...] = _split_maps(q_ref[...])
    m_ref[...] = jnp.full(m_ref.shape, -jnp.inf, F32)
    acc_ref[...] = jnp.zeros(acc_ref.shape, F32)
    chunks = [slice(c * rc, (c + 1) * rc) for c in range(2 * tq // rc)]
    rep = tq // LANES

    def scores(j, s_ref):
        kblk = k_ref[pl.ds(pl.multiple_of(j * tq, tq), tq), :]
        for rows in chunks:
            s_ref[rows, :] = lax.dot_general(qs_ref[rows, :], kblk, NT_DIMS, preferred_element_type=F32)

    def softmax_pv(j, s_ref, masked):
        vblk = v_ref[pl.ds(pl.multiple_of(j * tq, tq), tq), :]
        vext = jnp.concatenate([vblk, jnp.ones_like(vblk)], axis=1)
        for rows in chunks:
            s = s_ref[rows, :]
            if masked:
                r = lax.broadcasted_iota(jnp.int32, s.shape, 0) + (rows.start % tq)
                c = lax.broadcasted_iota(jnp.int32, s.shape, 1)
                s = jnp.where((c // CHUNK) <= (r // CHUNK), s, -jnp.inf)
            m_old = m_ref[rows, :]
            m_new = jnp.maximum(m_old, jnp.max(s, axis=-1, keepdims=True))
            alpha = jnp.exp(m_old - m_new)
            p = jnp.exp(s - jnp.concatenate([m_new] * rep, axis=1))
            acc_ref[rows, :] = (jnp.concatenate([alpha, alpha], axis=1) * acc_ref[rows, :]
                                + jnp.dot(p.astype(BF16), vext, preferred_element_type=F32))
            m_ref[rows, :] = m_new

    scores(qi, s0_ref)
    softmax_pv(qi, s0_ref, True)

    @pl.when(qi > 0)
    def _():
        scores(0, s0_ref)

        def pair(i, carry):
            j = 2 * i
            scores(j + 1, s1_ref)
            softmax_pv(j, s0_ref, False)
            scores(j + 2, s0_ref)
            softmax_pv(j + 1, s1_ref, False)
            return carry

        lax.fori_loop(0, qi // 2, pair, 0)

        @pl.when(qi % 2 == 1)
        def _():
            softmax_pv(qi - 1, s0_ref, False)

    o_ref[...] = _subln(acc_ref[:, 0:DV_B], acc_ref[:, DV_B:2 * DV_B], lam_ref[0], sw_ref[...], tq,
                        lam_init).astype(o_ref.dtype)


def _attn_prompt(lam, q, k, v, sw, tq, lam_init):
    T = q.shape[0]
    return pl.pallas_call(
        functools.partial(_attn_prompt_kernel, tq=tq, rc=min(512, tq), lam_init=lam_init),
        out_shape=jax.ShapeDtypeStruct((T, W_B), BF16),
        grid=(H_B, T // tq),
        in_specs=[pl.BlockSpec(memory_space=pltpu.SMEM),
                  pl.BlockSpec((tq, DV_B), lambda h, i: (i, h)),
                  pl.BlockSpec((T, DV_B), lambda h, i: (0, h)),
                  pl.BlockSpec((T, DV_B), lambda h, i: (0, h)),
                  pl.BlockSpec((1, DV_B), lambda h, i: (0, 0))],
        out_specs=pl.BlockSpec((tq, DV_B), lambda h, i: (i, h)),
        scratch_shapes=[pltpu.VMEM((2 * tq, DV_B), BF16), pltpu.VMEM((2 * tq, tq), F32),
                        pltpu.VMEM((2 * tq, tq), F32),
                        pltpu.VMEM((2 * tq, LANES), F32), pltpu.VMEM((2 * tq, 2 * DV_B), F32)],
        compiler_params=pltpu.CompilerParams(dimension_semantics=("arbitrary", "arbitrary"),
                                             vmem_limit_bytes=VMEM_LIMIT),
        name="attn_prompt",
    )(lam, q, k, v, sw)


def _attn_sample_kernel(lam_ref, q_ref, kn_ref, vn_ref, ck_ref, cv_ref, sw_ref, o_ref, *, tq, lam_init):
    for h in range(H_B):
        cols = slice(h * DV_B, (h + 1) * DV_B)
        qs = _split_maps(q_ref[:, cols])
        kc = ck_ref[:, h, :].astype(BF16)
        vc = cv_ref[:, h, :].astype(BF16)
        sc = lax.dot_general(qs, kc, NT_DIMS, preferred_element_type=F32)
        sn = lax.dot_general(qs, kn_ref[:, cols], NT_DIMS, preferred_element_type=F32)
        m = jnp.maximum(jnp.max(sc, axis=-1, keepdims=True), jnp.max(sn, axis=-1, keepdims=True))
        pc = jnp.exp(sc - m)
        pn = jnp.exp(sn - m)
        l = jnp.sum(pc, axis=-1, keepdims=True) + jnp.sum(pn, axis=-1, keepdims=True)
        acc = (jnp.dot(pc.astype(BF16), vc, preferred_element_type=F32)
               + jnp.dot(pn.astype(BF16), vn_ref[:, cols], preferred_element_type=F32))
        o_ref[:, cols] = _subln(acc, l, lam_ref[0], sw_ref[...], tq, lam_init).astype(o_ref.dtype)


def _attn_sample(lam, q, kn, vn, ck, cv, sw, lam_init):
    B, tq, _ = q.shape
    P = ck.shape[1]
    new = pl.BlockSpec((None, tq, W_B), lambda b: (b, 0, 0))
    cache = pl.BlockSpec((None, P, H_B, DV_B), lambda b: (b, 0, 0, 0))
    return pl.pallas_call(
        functools.partial(_attn_sample_kernel, tq=tq, lam_init=lam_init),
        out_shape=jax.ShapeDtypeStruct((B, tq, W_B), BF16),
        grid=(B,),
        in_specs=[pl.BlockSpec(memory_space=pltpu.SMEM), new, new, new, cache, cache,
                  pl.BlockSpec((1, DV_B), lambda b: (0, 0))],
        out_specs=new,
        compiler_params=pltpu.CompilerParams(dimension_semantics=("arbitrary",), vmem_limit_bytes=VMEM_LIMIT),
        name="attn_sample",
    )(lam, q, kn, vn, ck, cv, sw)


def _outproj_kernel(x_ref, oa_ref, ob_ref, w_ref, y_ref):
    y_ref[...] = (x_ref[...]
                  + jnp.dot(oa_ref[...], w_ref[0:W_A, :], preferred_element_type=F32)
                  + jnp.dot(ob_ref[...], w_ref[W_A:W_A + W_B, :], preferred_element_type=F32))


def _outproj(x2d, oa, ob, w_out, tm):
    T = x2d.shape[0]
    row = lambda c: pl.BlockSpec((tm, c), lambda i: (i, 0))
    return pl.pallas_call(
        _outproj_kernel,
        out_shape=jax.ShapeDtypeStruct((T, D_MODEL), F32),
        grid=(T // tm,),
        in_specs=[row(D_MODEL), row(W_A), row(W_B), pl.BlockSpec(w_out.shape, lambda i: (0, 0))],
        out_specs=row(D_MODEL),
        compiler_params=pltpu.CompilerParams(dimension_semantics=("arbitrary",), vmem_limit_bytes=VMEM_LIMIT),
        name="outproj",
    )(x2d, oa, ob, w_out)


_NEG_INF = float("-inf")
_PAIR_SLAB = SUBLANES
PEER_SLOTS = PEER_HEADS * PEER_TOPK


def _top16(s):
    n = s.shape[0]
    iota = lax.broadcasted_iota(jnp.int32, s.shape, 0)
    vals, idxs = [], []
    for _ in range(PEER_TOPK):
        m = jnp.max(s, axis=0, keepdims=True)
        first = jnp.min(jnp.where(s == m, iota, n), axis=0, keepdims=True)
        s = jnp.where(iota == first, _NEG_INF, s)
        vals.append(m)
        idxs.append(first.astype(F32))
    return vals, idxs


def _top16_pairs(v1, v2):
    t = v1[0].shape[1]
    v1m = jnp.concatenate(v1, axis=0)
    v2m = jnp.concatenate(v2, axis=0)
    v2s = v2m[0:_PAIR_SLAB]
    i16 = lax.broadcasted_iota(jnp.int32, (PEER_TOPK, t), 0)
    i8 = lax.broadcasted_iota(jnp.int32, (_PAIR_SLAB, t), 0)
    slabs = [v1[0] + v2m, v1[1] + v2s]
    flats = [i16, PEER_TOPK + i8]
    for p in range(2, _PAIR_SLAB):
        slabs.append(jnp.where(i8 < PEER_TOPK // (p + 1), v1[p] + v2s, _NEG_INF))
        flats.append(PEER_TOPK * p + i8)
    slabs.append(v1m[_PAIR_SLAB:] + v2[0])
    flats.append(PEER_TOPK * (_PAIR_SLAB + i8))
    c = jnp.concatenate(slabs, axis=0)
    flat = jnp.concatenate(flats, axis=0)
    big = PEER_TOPK * PEER_TOPK
    sums, picks = [], []
    for _ in range(PEER_TOPK):
        m = jnp.max(c, axis=0, keepdims=True)
        first = jnp.min(jnp.where(c == m, flat, big), axis=0, keepdims=True)
        c = jnp.where(flat == first, _NEG_INF, c)
        sums.append(m)
        picks.append(first)
    pick = jnp.concatenate(picks, axis=0)
    p_sel = jnp.right_shift(pick, 4).astype(F32)
    q_sel = jnp.bitwise_and(pick, PEER_TOPK - 1).astype(F32)
    return jnp.concatenate(sums, axis=0), p_sel, q_sel


def _lookup(ranks, rows):
    out = jnp.zeros(ranks.shape, F32)
    for p in range(PEER_TOPK):
        out = jnp.where(ranks == float(p), rows[p], out)
    return out


def _peer_score_kernel(x_ref, n2_ref, wq_ref, kblk_ref, h_ref, i1_ref, i2_ref, w_ref,
                       st_ref, a_ref, b_ref, g_ref, *, tt):
    x = x_ref[...]
    ms = jnp.mean(x * x, axis=-1, keepdims=True)
    h = (x * lax.rsqrt(ms + EPS) * n2_ref[...]).astype(BF16)
    h_ref[...] = h
    q = jnp.dot(h, wq_ref[...], preferred_element_type=F32)
    qb = q.astype(BF16)
    for hd in range(PEER_HEADS):
        st_ref[hd * 2 * N_KEYS:(hd + 1) * 2 * N_KEYS, :] = lax.dot_general(
            kblk_ref[...], qb[:, hd * 2 * PEER_DK_HALF:(hd + 1) * 2 * PEER_DK_HALF], NT_DIMS,
            preferred_element_type=F32)

    def head(hd, carry):
        r1 = pl.multiple_of(hd * (2 * N_KEYS), 2 * N_KEYS)
        slot = pl.multiple_of(hd * PEER_TOPK, PEER_TOPK)
        for g in range(tt // LANES):
            cols = slice(g * LANES, (g + 1) * LANES)
            v1, i1 = _top16(st_ref[pl.ds(r1, N_KEYS), cols])
            v2, i2 = _top16(st_ref[pl.ds(r1 + N_KEYS, N_KEYS), cols])
            sums, p_sel, q_sel = _top16_pairs(v1, v2)
            e = jnp.exp(sums - sums[0:1])
            a_ref[pl.ds(slot, PEER_TOPK), cols] = _lookup(p_sel, i1)
            b_ref[pl.ds(slot, PEER_TOPK), cols] = _lookup(q_sel, i2)
            g_ref[pl.ds(slot, PEER_TOPK), cols] = e / jnp.sum(e, axis=0, keepdims=True)
        return carry

    lax.fori_loop(0, PEER_HEADS, head, 0)
    i1_ref[...] = a_ref[...].T
    i2_ref[...] = b_ref[...].T
    w_ref[...] = g_ref[...].T


def _peer_score(x1, n2, wq, kblk, tt):
    T = x1.shape[0]
    slots = jax.ShapeDtypeStruct((T, PEER_SLOTS), F32)
    sspec = pl.BlockSpec((tt, PEER_SLOTS), lambda i: (i, 0))
    return pl.pallas_call(
        functools.partial(_peer_score_kernel, tt=tt),
        out_shape=(jax.ShapeDtypeStruct((T, D_MODEL), BF16), slots, slots, slots),
        grid=(T // tt,),
        in_specs=[pl.BlockSpec((tt, D_MODEL), lambda i: (i, 0)),
                  pl.BlockSpec(n2.shape, lambda i: (0, 0)),
                  pl.BlockSpec(wq.shape, lambda i: (0, 0)),
                  pl.BlockSpec(kblk.shape, lambda i: (0, 0))],
        out_specs=(pl.BlockSpec((tt, D_MODEL), lambda i: (i, 0)), sspec, sspec, sspec),
        scratch_shapes=[pltpu.VMEM((2 * PEER_HEADS * N_KEYS, tt), F32)]
                       + [pltpu.VMEM((PEER_SLOTS, tt), F32)] * 3,
        compiler_params=pltpu.CompilerParams(dimension_semantics=("arbitrary",), vmem_limit_bytes=VMEM_LIMIT),
        name="peer_score",
    )(x1, n2, wq, kblk)


_GATE_GROUP = 2 * SUBLANES


def _peer_gate_kernel(i1_ref, i2_ref, w_ref, g_ref, *, tg):
    key = lax.broadcasted_iota(jnp.int32, (N_KEYS, PEER_SLOTS), 0).astype(F32)

    def group(gi, carry):
        t0 = pl.multiple_of(gi * _GATE_GROUP, _GATE_GROUP)
        per_token = []
        for s in range(_GATE_GROUP):
            i1 = i1_ref[pl.ds(t0 + s, 1), :]
            i2 = i2_ref[pl.ds(t0 + s, 1), :]
            w = w_ref[pl.ds(t0 + s, 1), :]
            lhs = jnp.where(key == i1, w, 0.0).astype(BF16)
            rhs = jnp.where(key == i2, 1.0, 0.0).astype(BF16)
            per_token.append(lax.dot_general(lhs, rhs, NT_DIMS, preferred_element_type=F32))
        by_key = jnp.swapaxes(jnp.stack(per_token, axis=0), 0, 1)
        for a in range(N_KEYS):
            g_ref[pl.ds(t0, _GATE_GROUP), a * N_KEYS:(a + 1) * N_KEYS] = by_key[a].astype(BF16)
        return carry

    lax.fori_loop(0, tg // _GATE_GROUP, group, 0)


def _peer_gate(i1, i2, w, tg):
    T = i1.shape[0]
    sspec = pl.BlockSpec((tg, PEER_SLOTS), lambda i: (i, 0))
    return pl.pallas_call(
        functools.partial(_peer_gate_kernel, tg=tg),
        out_shape=jax.ShapeDtypeStruct((T, N_EXPERTS), BF16),
        grid=(T // tg,),
        in_specs=[sspec, sspec, sspec],
        out_specs=pl.BlockSpec((tg, N_EXPERTS), lambda i: (i, 0)),
        compiler_params=pltpu.CompilerParams(dimension_semantics=("arbitrary",), vmem_limit_bytes=VMEM_LIMIT),
        name="peer_gate",
    )(i1, i2, w)


def _peer_expert_kernel(h_ref, ut_ref, v_ref, g_ref, x_ref, nf_ref, y_ref, acc_ref):
    e = pl.program_id(1)

    @pl.when(e == 0)
    def _():
        acc_ref[...] = jnp.zeros(acc_ref.shape, F32)

    x = jnp.dot(h_ref[...], ut_ref[...], preferred_element_type=F32)
    act = 0.5 * x * (1.0 + lax.erf(x * (2.0 ** -0.5)))
    ga = (g_ref[...].astype(F32) * act).astype(BF16)
    acc_ref[...] += jnp.dot(ga, v_ref[...], preferred_element_type=F32)

    @pl.when(e == pl.num_programs(1) - 1)
    def _():
        xo = x_ref[...] + acc_ref[...]
        ms = jnp.mean(xo * xo, axis=-1, keepdims=True)
        y_ref[...] = xo * lax.rsqrt(ms + EPS) * nf_ref[...]


def _peer_expert(h, ut, v, g, x1, nf, tt, et):
    T = x1.shape[0]
    return pl.pallas_call(
        _peer_expert_kernel,
        out_shape=jax.ShapeDtypeStruct((T, D_MODEL), F32),
        grid=(T // tt, N_EXPERTS // et),
        in_specs=[pl.BlockSpec((tt, D_MODEL), lambda i, e: (i, 0)),
                  pl.BlockSpec((None, D_MODEL, et), lambda i, e: (0, 0, e)),
                  pl.BlockSpec((None, et, D_MODEL), lambda i, e: (0, e, 0)),
                  pl.BlockSpec((tt, et), lambda i, e: (i, e)),
                  pl.BlockSpec((tt, D_MODEL), lambda i, e: (i, 0), pipeline_mode=pl.Buffered(1)),
                  pl.BlockSpec(nf.shape, lambda i, e: (0, 0))],
        out_specs=pl.BlockSpec((tt, D_MODEL), lambda i, e: (i, 0)),
        scratch_shapes=[pltpu.VMEM((tt, D_MODEL), F32)],
        compiler_params=pltpu.CompilerParams(dimension_semantics=("arbitrary", "arbitrary"),
                                             vmem_limit_bytes=VMEM_LIMIT),
        name="peer_expert",
    )(h, ut, v, g, x1, nf)


def _pick_tile(T, prefs):
    for t in prefs:
        if T % t == 0:
            return t
    raise ValueError(f"no tile for {T}")


def _layer(x, conv_prev, ssm_prev, k_cache, v_cache, lam, lam_init, wts):
    (n1, w_main, w_ba, w_bat, conv_w, arow, drow, acol, dcol, gnw, sw, w_out, n2, wq, kblk, ut, v, nf) = wts
    B, T, _ = x.shape
    n_tok = B * T
    x2d = x.reshape(n_tok, D_MODEL)
    tm = _pick_tile(n_tok, (256, 128))
    conv_in, z, qb, kb, vb, kbh, vbh, ba, bat = _inproj(x2d, n1, w_main, w_ba, w_bat, tm)

    C = min(CHUNK, T)
    prev8 = jnp.pad(conv_prev.astype(F32), ((0, 0), (SUBLANES - CONV_W + 1, 0), (0, 0)))
    bat4 = bat.reshape(SUBLANES, B, T // C, C).transpose(1, 2, 0, 3)
    o_a, ssm_new = _gdn(conv_in.reshape(B, T, CONV_CH), z.reshape(B, T, W_A), ba.reshape(B, T, LANES), bat4,
                        prev8, ssm_prev.astype(F32), conv_w, arow, drow, acol, dcol, gnw, C,
                        _pick_tile(T // C, (4, 2, 1)))
    conv_new = conv_in.reshape(B, T, CONV_CH)[:, T - (CONV_W - 1):, :]

    if k_cache is None:
        tq = _pick_tile(T, (1024, 512, 256, 128))
        o_b = _attn_prompt(lam, qb, kbh, vbh, sw, tq, lam_init)
    else:
        o_b = _attn_sample(lam, qb.reshape(B, T, W_B), kbh.reshape(B, T, W_B), vbh.reshape(B, T, W_B),
                           k_cache, v_cache, sw, lam_init)
        o_b = o_b.reshape(n_tok, W_B)

    x1 = _outproj(x2d, o_a.reshape(n_tok, W_A), o_b, w_out, tm)

    tt = _pick_tile(n_tok, (512, 256, 128))
    h2, i1, i2, gate = _peer_score(x1, n2, wq, kblk, tt)
    g = _peer_gate(i1, i2, gate, _pick_tile(n_tok, (128,)))
    y = _peer_expert(h2, ut, v, g, x1, nf, _pick_tile(n_tok, (1024, 512, 256, 128)), PEER_NA * N_KEYS)
    return (y.reshape(B, T, D_MODEL), kb.reshape(B, T, H_B, 2 * DH_B), vb.reshape(B, T, H_B, DV_B),
            conv_new, ssm_new)


def _prep_weights(norm1_w, w_in, conv_w, a_log, dt_bias, gdn_norm_w, subln_w, w_out,
                  norm2_w, peer_wq, peer_k1, peer_k2, peer_u, peer_v, norm_f_w):
    o_beta = CONV_CH + W_A
    w_main = jnp.concatenate([w_in[:, :o_beta], w_in[:, o_beta + 2 * H_A:]], axis=1).astype(BF16)
    w_gate = w_in[:, o_beta:o_beta + 2 * H_A]
    w_ba = jnp.pad(w_gate, ((0, 0), (0, LANES - 2 * H_A))).astype(BF16)
    w_bat = w_gate.T.astype(BF16)
    zeros4 = jnp.zeros((H_A,), F32)
    a8 = jnp.concatenate([zeros4, a_log.astype(F32)])
    d8 = jnp.concatenate([zeros4, dt_bias.astype(F32)])
    arow = jnp.pad(a8, (0, LANES - 2 * H_A)).reshape(1, LANES)
    drow = jnp.pad(d8, (0, LANES - 2 * H_A)).reshape(1, LANES)
    acol = a8.reshape(2 * H_A, 1)
    dcol = d8.reshape(2 * H_A, 1)
    halves = jnp.stack([peer_k1, peer_k2]).astype(F32)
    eye = jnp.eye(2, dtype=F32)
    kblk = jnp.einsum("gj,gnd->gnjd", eye, halves).reshape(2 * N_KEYS, 2 * PEER_DK_HALF).astype(BF16)
    return (norm1_w.reshape(1, D_MODEL), w_main, w_ba, w_bat, conv_w, arow, drow, acol, dcol,
            gdn_norm_w.reshape(1, DK_A), subln_w.reshape(1, DV_B), w_out.astype(BF16),
            norm2_w.reshape(1, D_MODEL), peer_wq.astype(BF16), kblk,
            jnp.swapaxes(peer_u, 1, 2).astype(BF16), peer_v.astype(BF16), norm_f_w.reshape(1, D_MODEL))


def kernel(x_prompt, x_sample, cache_k, cache_v, state_conv, state_gdn, norm1_w, w_in, conv_w, a_log, dt_bias, gdn_norm_w, lam_q1, lam_k1, lam_q2, lam_k2, subln_w, w_out, norm2_w, peer_wq, peer_k1, peer_k2, peer_u, peer_v, norm_f_w):
    depth = w_in.shape[0]
    assert depth == 1, "the final norm is fused into the (single) layer's last kernel"
    l = 0
    lam_init = 0.8 - 0.6 * math.exp(-0.3 * l)
    lam = (jnp.exp(jnp.sum(lam_q1[l].astype(F32) * lam_k1[l].astype(F32)))
           - jnp.exp(jnp.sum(lam_q2[l].astype(F32) * lam_k2[l].astype(F32))) + lam_init).reshape(1)
    layer = lambda a: a.reshape(a.shape[1:])
    wts = _prep_weights(layer(norm1_w), layer(w_in), layer(conv_w), layer(a_log), layer(dt_bias),
                        layer(gdn_norm_w), layer(subln_w), layer(w_out), layer(norm2_w), layer(peer_wq),
                        layer(peer_k1), layer(peer_k2), peer_u, peer_v, norm_f_w)
    bp = x_prompt.shape[0]
    conv0 = jnp.zeros((bp, CONV_W - 1, CONV_CH), F32)
    ssm0 = jnp.zeros((bp, H_A, DK_A, DK_A), F32)
    yp, k1, v1, c1, s1 = _layer(x_prompt, conv0, ssm0, None, None, lam, lam_init, wts)
    ys, k2, v2, c2, s2 = _layer(x_sample, layer(state_conv), layer(state_gdn), layer(cache_k), layer(cache_v),
                                lam, lam_init, wts)
    st = lambda a: a[None]
    return (yp, ys, st(k1), st(v1), st(c1), st(s1), st(k2), st(v2), st(c2), st(s2))
```

```python
import functools
import math

import jax
import jax.numpy as jnp
from jax import lax
from jax.experimental import pallas as pl
from jax.experimental.pallas import tpu as pltpu

F32 = jnp.float32
BF16 = jnp.bfloat16
HIGHEST = lax.Precision.HIGHEST

D_MODEL = 1024
CHUNK = 64
H_A = 4
DK_A = 128
CONV_W = 4
W_A = H_A * DK_A
CONV_CH = 3 * W_A
H_B = 4
DH_B = 64
DV_B = 128
W_B = H_B * DV_B
DIFF_EPS = 1e-5
N_KEYS = 128
N_EXPERTS = N_KEYS * N_KEYS
PEER_HEADS = 8
PEER_TOPK = 16
PEER_DK_HALF = 64
PEER_NA = 16
EPS = 1e-6
LANES = 128
SUBLANES = 8
VMEM_LIMIT = 56 * 1024 * 1024

NT_DIMS = (((1,), (1,)), ((), ()))
TN_DIMS = (((0,), (0,)), ((), ()))


def _sigmoid(x):
    return 1.0 / (1.0 + jnp.exp(-x))


def _softplus(x):
    return jnp.maximum(x, 0.0) + jnp.log(1.0 + jnp.exp(-jnp.abs(x)))


def _dot_hi(a, b):
    return jnp.dot(a, b, precision=HIGHEST, preferred_element_type=F32)


def _dot_bf(a, b, dims=(((1,), (0,)), ((), ()))):
    return lax.dot_general(a.astype(BF16), b.astype(BF16), dims, preferred_element_type=F32)


def _split_bf(a):
    hi = a.astype(BF16)
    return hi, (a - hi.astype(F32)).astype(BF16)


def _dot_split(a, b):
    d = lambda x, y: jnp.dot(x, y, preferred_element_type=F32)
    return d(a[0], b[0]) + (d(a[0], b[1]) + d(a[1], b[0]))


def _inproj_kernel(x_ref, n1_ref, w_ref, wba_ref, wbat_ref,
                   conv_ref, z_ref, qb_ref, kb_ref, vb_ref, kbh_ref, vbh_ref, ba_ref, bat_ref):
    x = x_ref[...]
    ms = jnp.mean(x * x, axis=-1, keepdims=True)
    h = (x * lax.rsqrt(ms + EPS) * n1_ref[...]).astype(BF16)

    def mm(c0, c1):
        return jnp.dot(h, w_ref[:, c0:c1], preferred_element_type=F32)

    conv_ref[...] = mm(0, CONV_CH)
    z_ref[...] = mm(CONV_CH, CONV_CH + W_A)
    o = CONV_CH + W_A
    qb_ref[...] = (mm(o, o + W_B) * (DH_B ** -0.5)).astype(BF16)
    kb = mm(o + W_B, o + 2 * W_B)
    kb_ref[...] = kb
    kbh_ref[...] = kb.astype(BF16)
    vb = mm(o + 2 * W_B, o + 3 * W_B)
    vb_ref[...] = vb
    vbh_ref[...] = vb.astype(BF16)
    ba_ref[...] = jnp.dot(h, wba_ref[...], preferred_element_type=F32)
    bat_ref[...] = lax.dot_general(wbat_ref[...], h, NT_DIMS, preferred_element_type=F32)


def _inproj(x2d, n1, w_main, w_ba, w_bat, tm):
    T = x2d.shape[0]
    row = lambda c: pl.BlockSpec((tm, c), lambda i: (i, 0))
    full = lambda a: pl.BlockSpec(a.shape, lambda i: (0, 0))
    out_shape = (
        jax.ShapeDtypeStruct((T, CONV_CH), F32),
        jax.ShapeDtypeStruct((T, W_A), F32),
        jax.ShapeDtypeStruct((T, W_B), BF16),
        jax.ShapeDtypeStruct((T, W_B), F32),
        jax.ShapeDtypeStruct((T, W_B), F32),
        jax.ShapeDtypeStruct((T, W_B), BF16),
        jax.ShapeDtypeStruct((T, W_B), BF16),
        jax.ShapeDtypeStruct((T, LANES), F32),
        jax.ShapeDtypeStruct((SUBLANES, T), F32),
    )
    out_specs = (row(CONV_CH), row(W_A), row(W_B), row(W_B), row(W_B), row(W_B), row(W_B), row(LANES),
                 pl.BlockSpec((SUBLANES, tm), lambda i: (0, i)))
    return pl.pallas_call(
        _inproj_kernel,
        out_shape=out_shape,
        grid=(T // tm,),
        in_specs=[row(D_MODEL), full(n1), full(w_main), full(w_ba), full(w_bat)],
        out_specs=out_specs,
        compiler_params=pltpu.CompilerParams(dimension_semantics=("arbitrary",), vmem_limit_bytes=VMEM_LIMIT),
        name="inproj",
    )(x2d, n1, w_main, w_ba, w_bat)


def _gdn_kernel(conv_ref, z_ref, ba_ref, bat_ref, prev_ref, s0_ref, cw_ref, arow_ref, drow_ref,
                acol_ref, dcol_ref, nw_ref, o_ref, sout_ref, up_ref, s_ref, *, C, nc):
    t = pl.program_id(1)
    tt = nc * C

    @pl.when(t == 0)
    def _():
        up_ref[0:SUBLANES, :] = prev_ref[...]
        s_ref[...] = s0_ref[...]

    up_ref[SUBLANES:SUBLANES + tt, :] = conv_ref[...]
    y = up_ref[pl.ds(SUBLANES - CONV_W + 1, tt), :] * cw_ref[0:1, :]
    for j in range(1, CONV_W):
        y = y + up_ref[pl.ds(SUBLANES - CONV_W + 1 + j, tt), :] * cw_ref[j:j + 1, :]
    qkv = y * _sigmoid(y)
    tail = up_ref[tt:tt + SUBLANES, :]
    up_ref[0:SUBLANES, :] = tail

    ri = lax.broadcasted_iota(jnp.int32, (C, C), 0)
    ci = lax.broadcasted_iota(jnp.int32, (C, C), 1)
    incl = ri >= ci
    strict = ri > ci
    tril = incl.astype(F32)
    triu = (ri <= ci).astype(F32)
    eye = (ri == ci).astype(F32)

    ba = ba_ref[...]
    beta_all = _sigmoid(ba)
    g_all = -jnp.exp(arow_ref[...]) * _softplus(ba + drow_ref[...])
    nw = nw_ref[...]

    probs = []
    for c in range(nc):
        rows = slice(c * C, (c + 1) * C)
        gc_all = _dot_hi(tril, g_all[rows])
        g_t = -jnp.exp(acol_ref[...]) * _softplus(bat_ref[c] + dcol_ref[...])
        gc_t = _dot_hi(g_t, triu)
        for h in range(H_A):
            gcc = gc_all[:, H_A + h:H_A + h + 1]
            gcr = gc_t[H_A + h:H_A + h + 1, :]
            beta = beta_all[rows, h:h + 1]
            decay = jnp.where(incl, jnp.exp(jnp.where(incl, gcc - gcr, 0.0)), 0.0)
            qh = qkv[rows, h * DK_A:(h + 1) * DK_A]
            kh = qkv[rows, W_A + h * DK_A:W_A + (h + 1) * DK_A]
            vh = qkv[rows, 2 * W_A + h * DK_A:2 * W_A + (h + 1) * DK_A]
            qh = qh * lax.rsqrt(jnp.sum(qh * qh, axis=-1, keepdims=True) + EPS) * (DK_A ** -0.5)
            kh = kh * lax.rsqrt(jnp.sum(kh * kh, axis=-1, keepdims=True) + EPS)
            kb = kh * beta
            n = jnp.where(strict, -(_dot_bf(kb, kh, NT_DIMS) * decay), 0.0)
            egc = jnp.exp(gcc)
            g_last = gcc[C - 1:C, :]
            probs.append(dict(h=h, rows=rows, n=n, vb=(vh * beta).astype(BF16),
                              kbe=(kb * egc).astype(BF16), qg=(qh * egc).astype(BF16),
                              attn=(_dot_bf(qh, kh, NT_DIMS) * decay).astype(BF16),
                              kd=(kh * jnp.exp(g_last - gcc)).astype(BF16), eg=jnp.exp(g_last)))

    tinv = [eye + pr["n"] for pr in probs]
    pw = [_split_bf(pr["n"]) for pr in probs]
    for _ in range(int(math.log2(C)) - 1):
        pw = [_split_bf(_dot_split(p, p)) for p in pw]
        tinv = [ti + _dot_split(_split_bf(ti), p) for ti, p in zip(tinv, pw)]
    for pr, ti in zip(probs, tinv):
        ti = ti.astype(BF16)
        pr["u"] = jnp.dot(ti, pr["vb"], preferred_element_type=F32)
        pr["w"] = jnp.dot(ti, pr["kbe"], preferred_element_type=F32).astype(BF16)

    state = [s_ref[h] for h in range(H_A)]
    for pr in probs:
        h, rows = pr["h"], pr["rows"]
        s = state[h]
        s_bf = s.astype(BF16)
        v_new = (pr["u"] - jnp.dot(pr["w"], s_bf, preferred_element_type=F32)).astype(BF16)
        o = (jnp.dot(pr["qg"], s_bf, preferred_element_type=F32)
             + jnp.dot(pr["attn"], v_new, preferred_element_type=F32))
        state[h] = s * pr["eg"] + lax.dot_general(pr["kd"], v_new, TN_DIMS, preferred_element_type=F32)
        on = o * lax.rsqrt(jnp.mean(o * o, axis=-1, keepdims=True) + EPS) * nw
        zz = z_ref[rows, h * DK_A:(h + 1) * DK_A]
        o_ref[rows, h * DK_A:(h + 1) * DK_A] = (on * (zz * _sigmoid(zz))).astype(o_ref.dtype)
    for h in range(H_A):
        s_ref[h] = state[h]

    @pl.when(t == pl.num_programs(1) - 1)
    def _():
        sout_ref[...] = s_ref[...]


def _gdn(conv_in, z, ba, bat, prev8, s0, conv_w, arow, drow, acol, dcol, nw, C, nc):
    B, T, _ = conv_in.shape
    tt = nc * C
    tok = lambda c: pl.BlockSpec((None, tt, c), lambda b, t: (b, t, 0))
    full2 = lambda a: pl.BlockSpec(a.shape, lambda b, t: (0, 0))
    return pl.pallas_call(
        functools.partial(_gdn_kernel, C=C, nc=nc),
        out_shape=(jax.ShapeDtypeStruct((B, T, W_A), BF16),
                   jax.ShapeDtypeStruct((B, H_A, DK_A, DK_A), F32)),
        grid=(B, T // tt),
        in_specs=[tok(CONV_CH), tok(W_A), tok(LANES),
                  pl.BlockSpec((None, nc, SUBLANES, C), lambda b, t: (b, t, 0, 0)),
                  pl.BlockSpec((None, SUBLANES, CONV_CH), lambda b, t: (b, 0, 0)),
                  pl.BlockSpec((None, H_A, DK_A, DK_A), lambda b, t: (b, 0, 0, 0)),
                  full2(conv_w), full2(arow), full2(drow), full2(acol), full2(dcol), full2(nw)],
        out_specs=(tok(W_A), pl.BlockSpec((None, H_A, DK_A, DK_A), lambda b, t: (b, 0, 0, 0))),
        scratch_shapes=[pltpu.VMEM((SUBLANES + tt, CONV_CH), F32), pltpu.VMEM((H_A, DK_A, DK_A), F32)],
        compiler_params=pltpu.CompilerParams(dimension_semantics=("arbitrary", "arbitrary"),
                                             vmem_limit_bytes=VMEM_LIMIT),
        name="gdn",
    )(conv_in, z, ba, bat, prev8, s0, conv_w, arow, drow, acol, dcol, nw)


def _split_maps(q):
    lane = lax.broadcasted_iota(jnp.int32, q.shape, 1)
    zero = jnp.zeros_like(q)
    return jnp.concatenate([jnp.where(lane < DH_B, q, zero), jnp.where(lane >= DH_B, q, zero)], axis=0)


def _subln(acc, l, lam, sw, tq, lam_init):
    o = acc[:tq] / l[:tq] - lam * (acc[tq:] / l[tq:])
    return o * lax.rsqrt(jnp.mean(o * o, axis=-1, keepdims=True) + DIFF_EPS) * sw * (1.0 - lam_init)


def _attn_prompt_kernel(lam_ref, q_ref, k_ref, v_ref, sw_ref, o_ref, qs_ref, s0_ref, s1_ref, m_ref,
                        acc_ref, *, tq, rc, lam_init):
    qi = pl.program_id(1)
    qs_ref[...] = _split_maps(q_ref[...])
    m_ref[...] = jnp.full(m_ref.shape, -jnp.inf, F32)
    acc_ref[...] = jnp.zeros(acc_ref.shape, F32)
    chunks = [slice(c * rc, (c + 1) * rc) for c in range(2 * tq // rc)]
    rep = tq // LANES

    def scores(j, s_ref):
        kblk = k_ref[pl.ds(pl.multiple_of(j * tq, tq), tq), :]
        for rows in chunks:
            s_ref[rows, :] = lax.dot_general(qs_ref[rows, :], kblk, NT_DIMS, preferred_element_type=F32)

    def softmax_pv(j, s_ref, masked):
        vblk = v_ref[pl.ds(pl.multiple_of(j * tq, tq), tq), :]
        vext = jnp.concatenate([vblk, jnp.ones_like(vblk)], axis=1)
        for rows in chunks:
            s = s_ref[rows, :]
            if masked:
                r = lax.broadcasted_iota(jnp.int32, s.shape, 0) + (rows.start % tq)
                c = lax.broadcasted_iota(jnp.int32, s.shape, 1)
                s = jnp.where((c // CHUNK) <= (r // CHUNK), s, -jnp.inf)
            m_old = m_ref[rows, :]
            m_new = jnp.maximum(m_old, jnp.max(s, axis=-1, keepdims=True))
            alpha = jnp.exp(m_old - m_new)
            p = jnp.exp(s - jnp.concatenate([m_new] * rep, axis=1))
            acc_ref[rows, :] = (jnp.concatenate([alpha, alpha], axis=1) * acc_ref[rows, :]
                                + jnp.dot(p.astype(BF16), vext, preferred_element_type=F32))
            m_ref[rows, :] = m_new

    scores(qi, s0_ref)
    softmax_pv(qi, s0_ref, True)

    @pl.when(qi > 0)
    def _():
        scores(0, s0_ref)

        def pair(i, carry):
            j = 2 * i
            scores(j + 1, s1_ref)
            softmax_pv(j, s0_ref, False)
            scores(j + 2, s0_ref)
            softmax_pv(j + 1, s1_ref, False)
            return carry

        lax.fori_loop(0, qi // 2, pair, 0)

        @pl.when(qi % 2 == 1)
        def _():
            softmax_pv(qi - 1, s0_ref, False)

    o_ref[...] = _subln(acc_ref[:, 0:DV_B], acc_ref[:, DV_B:2 * DV_B], lam_ref[0], sw_ref[...], tq,
                        lam_init).astype(o_ref.dtype)


def _attn_prompt(lam, q, k, v, sw, tq, lam_init):
    T = q.shape[0]
    return pl.pallas_call(
        functools.partial(_attn_prompt_kernel, tq=tq, rc=min(512, tq), lam_init=lam_init),
        out_shape=jax.ShapeDtypeStruct((T, W_B), BF16),
        grid=(H_B, T // tq),
        in_specs=[pl.BlockSpec(memory_space=pltpu.SMEM),
                  pl.BlockSpec((tq, DV_B), lambda h, i: (i, h)),
                  pl.BlockSpec((T, DV_B), lambda h, i: (0, h)),
                  pl.BlockSpec((T, DV_B), lambda h, i: (0, h)),
                  pl.BlockSpec((1, DV_B), lambda h, i: (0, 0))],
        out_specs=pl.BlockSpec((tq, DV_B), lambda h, i: (i, h)),
        scratch_shapes=[pltpu.VMEM((2 * tq, DV_B), BF16), pltpu.VMEM((2 * tq, tq), F32),
                        pltpu.VMEM((2 * tq, tq), F32),
                        pltpu.VMEM((2 * tq, LANES), F32), pltpu.VMEM((2 * tq, 2 * DV_B), F32)],
        compiler_params=pltpu.CompilerParams(dimension_semantics=("arbitrary", "arbitrary"),
                                             vmem_limit_bytes=VMEM_LIMIT),
        name="attn_prompt",
    )(lam, q, k, v, sw)


def _attn_sample_kernel(lam_ref, q_ref, kn_ref, vn_ref, ck_ref, cv_ref, sw_ref, o_ref, *, tq, lam_init):
    for h in range(H_B):
        cols = slice(h * DV_B, (h + 1) * DV_B)
        qs = _split_maps(q_ref[:, cols])
        kc = ck_ref[:, h, :].astype(BF16)
        vc = cv_ref[:, h, :].astype(BF16)
        sc = lax.dot_general(qs, kc, NT_DIMS, preferred_element_type=F32)
        sn = lax.dot_general(qs, kn_ref[:, cols], NT_DIMS, preferred_element_type=F32)
        m = jnp.maximum(jnp.max(sc, axis=-1, keepdims=True), jnp.max(sn, axis=-1, keepdims=True))
        pc = jnp.exp(sc - m)
        pn = jnp.exp(sn - m)
        l = jnp.sum(pc, axis=-1, keepdims=True) + jnp.sum(pn, axis=-1, keepdims=True)
        acc = (jnp.dot(pc.astype(BF16), vc, preferred_element_type=F32)
               + jnp.dot(pn.astype(BF16), vn_ref[:, cols], preferred_element_type=F32))
        o_ref[:, cols] = _subln(acc, l, lam_ref[0], sw_ref[...], tq, lam_init).astype(o_ref.dtype)


def _attn_sample(lam, q, kn, vn, ck, cv, sw, lam_init):
    B, tq, _ = q.shape
    P = ck.shape[1]
    new = pl.BlockSpec((None, tq, W_B), lambda b: (b, 0, 0))
    cache = pl.BlockSpec((None, P, H_B, DV_B), lambda b: (b, 0, 0, 0))
    return pl.pallas_call(
        functools.partial(_attn_sample_kernel, tq=tq, lam_init=lam_init),
        out_shape=jax.ShapeDtypeStruct((B, tq, W_B), BF16),
        grid=(B,),
        in_specs=[pl.BlockSpec(memory_space=pltpu.SMEM), new, new, new, cache, cache,
                  pl.BlockSpec((1, DV_B), lambda b: (0, 0))],
        out_specs=new,
        compiler_params=pltpu.CompilerParams(dimension_semantics=("arbitrary",), vmem_limit_bytes=VMEM_LIMIT),
        name="attn_sample",
    )(lam, q, kn, vn, ck, cv, sw)


def _outproj_kernel(x_ref, oa_ref, ob_ref, w_ref, y_ref):
    y_ref[...] = (x_ref[...]
                  + jnp.dot(oa_ref[...], w_ref[0:W_A, :], preferred_element_type=F32)
                  + jnp.dot(ob_ref[...], w_ref[W_A:W_A + W_B, :], preferred_element_type=F32))


def _outproj(x2d, oa, ob, w_out, tm):
    T = x2d.shape[0]
    row = lambda c: pl.BlockSpec((tm, c), lambda i: (i, 0))
    return pl.pallas_call(
        _outproj_kernel,
        out_shape=jax.ShapeDtypeStruct((T, D_MODEL), F32),
        grid=(T // tm,),
        in_specs=[row(D_MODEL), row(W_A), row(W_B), pl.BlockSpec(w_out.shape, lambda i: (0, 0))],
        out_specs=row(D_MODEL),
        compiler_params=pltpu.CompilerParams(dimension_semantics=("arbitrary",), vmem_limit_bytes=VMEM_LIMIT),
        name="outproj",
    )(x2d, oa, ob, w_out)


_NEG_INF = float("-inf")
_PAIR_SLAB = SUBLANES
PEER_SLOTS = PEER_HEADS * PEER_TOPK


def _top16(s):
    n = s.shape[0]
    iota = lax.broadcasted_iota(jnp.int32, s.shape, 0)
    vals, idxs = [], []
    for _ in range(PEER_TOPK):
        m = jnp.max(s, axis=0, keepdims=True)
        first = jnp.min(jnp.where(s == m, iota, n), axis=0, keepdims=True)
        s = jnp.where(iota == first, _NEG_INF, s)
        vals.append(m)
        idxs.append(first.astype(F32))
    return vals, idxs


def _batcher_pairs(lo, hi):
    def merge(lo, hi, r):
        step = 2 * r
        if step < hi - lo:
            yield from merge(lo, hi, step)
            yield from merge(lo + r, hi, step)
            yield from ((i, i + r) for i in range(lo + r, hi - r, step))
        else:
            yield (lo, lo + r)

    if hi - lo >= 1:
        mid = lo + (hi - lo) // 2
        yield from _batcher_pairs(lo, mid)
        yield from _batcher_pairs(mid + 1, hi)
        yield from merge(lo, hi, 1)


def _exchange(v, ix, i, j):
    keep = v[i] >= v[j]
    v[i], v[j] = jnp.maximum(v[i], v[j]), jnp.minimum(v[i], v[j])
    ix[i], ix[j] = jnp.where(keep, ix[i], ix[j]), jnp.where(keep, ix[j], ix[i])


def _top16_sorted(s):
    n, t = s.shape
    groups = n // SUBLANES
    sub = lax.broadcasted_iota(jnp.int32, (SUBLANES, t), 0).astype(F32)
    v = [s[g * SUBLANES:(g + 1) * SUBLANES] for g in range(groups)]
    ix = [sub + float(g * SUBLANES) for g in range(groups)]
    for i, j in _batcher_pairs(0, groups - 1):
        _exchange(v, ix, i, j)
    for shift in (SUBLANES // 2, SUBLANES // 4, 1):
        pv = [pltpu.roll(x, shift, 0) for x in v]
        pi = [pltpu.roll(x, shift, 0) for x in ix]
        for i in range(groups):
            j = groups - 1 - i
            keep = v[i] >= pv[j]
            v[i] = jnp.maximum(v[i], pv[j])
            ix[i] = jnp.where(keep, ix[i], pi[j])
        stride = groups // 2
        while stride >= 1:
            for i in range(groups):
                if i & stride == 0:
                    _exchange(v, ix, i, i + stride)
            stride //= 2
    tied = jnp.zeros((1, t), F32)
    for p in range(groups - 1):
        tied = jnp.maximum(tied, jnp.where(v[p][0:1] == v[p + 1][0:1], 1.0, 0.0))
    at_least = jnp.sum(jnp.where(s >= v[groups - 1][0:1], 1.0, 0.0), axis=0, keepdims=True)
    tied = jnp.maximum(tied, jnp.where(at_least != float(PEER_TOPK), 1.0, 0.0))
    return [x[0:1] for x in v], [x[0:1] for x in ix], tied


def _top16_pairs(v1, v2):
    t = v1[0].shape[1]
    v1m = jnp.concatenate(v1, axis=0)
    v2m = jnp.concatenate(v2, axis=0)
    v2s = v2m[0:_PAIR_SLAB]
    i16 = lax.broadcasted_iota(jnp.int32, (PEER_TOPK, t), 0)
    i8 = lax.broadcasted_iota(jnp.int32, (_PAIR_SLAB, t), 0)
    slabs = [v1[0] + v2m, v1[1] + v2s]
    flats = [i16, PEER_TOPK + i8]
    for p in range(2, _PAIR_SLAB):
        slabs.append(jnp.where(i8 < PEER_TOPK // (p + 1), v1[p] + v2s, _NEG_INF))
        flats.append(PEER_TOPK * p + i8)
    slabs.append(v1m[_PAIR_SLAB:] + v2[0])
    flats.append(PEER_TOPK * (_PAIR_SLAB + i8))
    c = jnp.concatenate(slabs, axis=0)
    flat = jnp.concatenate(flats, axis=0)
    big = PEER_TOPK * PEER_TOPK
    sums, picks = [], []
    for _ in range(PEER_TOPK):
        m = jnp.max(c, axis=0, keepdims=True)
        first = jnp.min(jnp.where(c == m, flat, big), axis=0, keepdims=True)
        c = jnp.where(flat == first, _NEG_INF, c)
        sums.append(m)
        picks.append(first)
    pick = jnp.concatenate(picks, axis=0)
    p_sel = jnp.right_shift(pick, 4).astype(F32)
    q_sel = jnp.bitwise_and(pick, PEER_TOPK - 1).astype(F32)
    return jnp.concatenate(sums, axis=0), p_sel, q_sel


def _lookup(ranks, rows):
    out = jnp.zeros(ranks.shape, F32)
    for p in range(PEER_TOPK):
        out = jnp.where(ranks == float(p), rows[p], out)
    return out


def _peer_score_kernel(x_ref, n2_ref, wq_ref, kblk_ref, h_ref, i1_ref, i2_ref, w_ref,
                       st_ref, a_ref, b_ref, g_ref, *, tt):
    x = x_ref[...]
    ms = jnp.mean(x * x, axis=-1, keepdims=True)
    h = (x * lax.rsqrt(ms + EPS) * n2_ref[...]).astype(BF16)
    h_ref[...] = h
    q = jnp.dot(h, wq_ref[...], preferred_element_type=F32)
    qb = q.astype(BF16)
    for hd in range(PEER_HEADS):
        st_ref[hd * 2 * N_KEYS:(hd + 1) * 2 * N_KEYS, :] = lax.dot_general(
            kblk_ref[...], qb[:, hd * 2 * PEER_DK_HALF:(hd + 1) * 2 * PEER_DK_HALF], NT_DIMS,
            preferred_element_type=F32)

    def retrieve(exact):
        def head(hd, tied):
            r1 = pl.multiple_of(hd * (2 * N_KEYS), 2 * N_KEYS)
            slot = pl.multiple_of(hd * PEER_TOPK, PEER_TOPK)
            for g in range(tt // LANES):
                cols = slice(g * LANES, (g + 1) * LANES)
                s1 = st_ref[pl.ds(r1, N_KEYS), cols]
                s2 = st_ref[pl.ds(r1 + N_KEYS, N_KEYS), cols]
                if exact:
                    (v1, i1), (v2, i2) = _top16(s1), _top16(s2)
                else:
                    v1, i1, t1 = _top16_sorted(s1)
                    v2, i2, t2 = _top16_sorted(s2)
                    tied = jnp.maximum(tied, jnp.maximum(t1, t2))
                sums, p_sel, q_sel = _top16_pairs(v1, v2)
                e = jnp.exp(sums - sums[0:1])
                a_ref[pl.ds(slot, PEER_TOPK), cols] = _lookup(p_sel, i1)
                b_ref[pl.ds(slot, PEER_TOPK), cols] = _lookup(q_sel, i2)
                g_ref[pl.ds(slot, PEER_TOPK), cols] = e / jnp.sum(e, axis=0, keepdims=True)
            return tied
        return head

    tied = lax.fori_loop(0, PEER_HEADS, retrieve(False), jnp.zeros((1, LANES), F32))

    @pl.when(jnp.max(tied) > 0.0)
    def _():
        lax.fori_loop(0, PEER_HEADS, retrieve(True), jnp.zeros((1, LANES), F32))

    i1_ref[...] = a_ref[...].T
    i2_ref[...] = b_ref[...].T
    w_ref[...] = g_ref[...].T


def _peer_score(x1, n2, wq, kblk, tt):
    T = x1.shape[0]
    slots = jax.ShapeDtypeStruct((T, PEER_SLOTS), F32)
    sspec = pl.BlockSpec((tt, PEER_SLOTS), lambda i: (i, 0))
    return pl.pallas_call(
        functools.partial(_peer_score_kernel, tt=tt),
        out_shape=(jax.ShapeDtypeStruct((T, D_MODEL), BF16), slots, slots, slots),
        grid=(T // tt,),
        in_specs=[pl.BlockSpec((tt, D_MODEL), lambda i: (i, 0)),
                  pl.BlockSpec(n2.shape, lambda i: (0, 0)),
                  pl.BlockSpec(wq.shape, lambda i: (0, 0)),
                  pl.BlockSpec(kblk.shape, lambda i: (0, 0))],
        out_specs=(pl.BlockSpec((tt, D_MODEL), lambda i: (i, 0)), sspec, sspec, sspec),
        scratch_shapes=[pltpu.VMEM((2 * PEER_HEADS * N_KEYS, tt), F32)]
                       + [pltpu.VMEM((PEER_SLOTS, tt), F32)] * 3,
        compiler_params=pltpu.CompilerParams(dimension_semantics=("arbitrary",), vmem_limit_bytes=VMEM_LIMIT),
        name="peer_score",
    )(x1, n2, wq, kblk)


_GATE_GROUP = 2 * SUBLANES


def _peer_gate_kernel(i1_ref, i2_ref, w_ref, g_ref, *, tg):
    key = lax.broadcasted_iota(jnp.int32, (N_KEYS, PEER_SLOTS), 0).astype(F32)

    def group(gi, carry):
        t0 = pl.multiple_of(gi * _GATE_GROUP, _GATE_GROUP)
        per_token = []
        for s in range(_GATE_GROUP):
            i1 = i1_ref[pl.ds(t0 + s, 1), :]
            i2 = i2_ref[pl.ds(t0 + s, 1), :]
            w = w_ref[pl.ds(t0 + s, 1), :]
            lhs = jnp.where(key == i1, w, 0.0).astype(BF16)
            rhs = jnp.where(key == i2, 1.0, 0.0).astype(BF16)
            per_token.append(lax.dot_general(lhs, rhs, NT_DIMS, preferred_element_type=F32))
        by_key = jnp.swapaxes(jnp.stack(per_token, axis=0), 0, 1)
        for a in range(N_KEYS):
            g_ref[pl.ds(t0, _GATE_GROUP), a * N_KEYS:(a + 1) * N_KEYS] = by_key[a].astype(BF16)
        return carry

    lax.fori_loop(0, tg // _GATE_GROUP, group, 0)


def _peer_gate(i1, i2, w, tg):
    T = i1.shape[0]
    sspec = pl.BlockSpec((tg, PEER_SLOTS), lambda i: (i, 0))
    return pl.pallas_call(
        functools.partial(_peer_gate_kernel, tg=tg),
        out_shape=jax.ShapeDtypeStruct((T, N_EXPERTS), BF16),
        grid=(T // tg,),
        in_specs=[sspec, sspec, sspec],
        out_specs=pl.BlockSpec((tg, N_EXPERTS), lambda i: (i, 0)),
        compiler_params=pltpu.CompilerParams(dimension_semantics=("arbitrary",), vmem_limit_bytes=VMEM_LIMIT),
        name="peer_gate",
    )(i1, i2, w)


def _peer_expert_kernel(h_ref, ut_ref, v_ref, g_ref, x_ref, nf_ref, y_ref, acc_ref):
    e = pl.program_id(1)

    @pl.when(e == 0)
    def _():
        acc_ref[...] = jnp.zeros(acc_ref.shape, F32)

    x = jnp.dot(h_ref[...], ut_ref[...], preferred_element_type=F32)
    act = 0.5 * x * (1.0 + lax.erf(x * (2.0 ** -0.5)))
    ga = (g_ref[...].astype(F32) * act).astype(BF16)
    acc_ref[...] += jnp.dot(ga, v_ref[...], preferred_element_type=F32)

    @pl.when(e == pl.num_programs(1) - 1)
    def _():
        xo = x_ref[...] + acc_ref[...]
        ms = jnp.mean(xo * xo, axis=-1, keepdims=True)
        y_ref[...] = xo * lax.rsqrt(ms + EPS) * nf_ref[...]


def _peer_expert(h, ut, v, g, x1, nf, tt, et):
    T = x1.shape[0]
    return pl.pallas_call(
        _peer_expert_kernel,
        out_shape=jax.ShapeDtypeStruct((T, D_MODEL), F32),
        grid=(T // tt, N_EXPERTS // et),
        in_specs=[pl.BlockSpec((tt, D_MODEL), lambda i, e: (i, 0)),
                  pl.BlockSpec((None, D_MODEL, et), lambda i, e: (0, 0, e)),
                  pl.BlockSpec((None, et, D_MODEL), lambda i, e: (0, e, 0)),
                  pl.BlockSpec((tt, et), lambda i, e: (i, e)),
                  pl.BlockSpec((tt, D_MODEL), lambda i, e: (i, 0), pipeline_mode=pl.Buffered(1)),
                  pl.BlockSpec(nf.shape, lambda i, e: (0, 0))],
        out_specs=pl.BlockSpec((tt, D_MODEL), lambda i, e: (i, 0)),
        scratch_shapes=[pltpu.VMEM((tt, D_MODEL), F32)],
        compiler_params=pltpu.CompilerParams(dimension_semantics=("arbitrary", "arbitrary"),
                                             vmem_limit_bytes=VMEM_LIMIT),
        name="peer_expert",
    )(h, ut, v, g, x1, nf)


def _pick_tile(T, prefs):
    for t in prefs:
        if T % t == 0:
            return t
    raise ValueError(f"no tile for {T}")


def _layer(x, conv_prev, ssm_prev, k_cache, v_cache, lam, lam_init, wts):
    (n1, w_main, w_ba, w_bat, conv_w, arow, drow, acol, dcol, gnw, sw, w_out, n2, wq, kblk, ut, v, nf) = wts
    B, T, _ = x.shape
    n_tok = B * T
    x2d = x.reshape(n_tok, D_MODEL)
    tm = _pick_tile(n_tok, (256, 128))
    conv_in, z, qb, kb, vb, kbh, vbh, ba, bat = _inproj(x2d, n1, w_main, w_ba, w_bat, tm)

    C = min(CHUNK, T)
    prev8 = jnp.pad(conv_prev.astype(F32), ((0, 0), (SUBLANES - CONV_W + 1, 0), (0, 0)))
    bat4 = bat.reshape(SUBLANES, B, T // C, C).transpose(1, 2, 0, 3)
    o_a, ssm_new = _gdn(conv_in.reshape(B, T, CONV_CH), z.reshape(B, T, W_A), ba.reshape(B, T, LANES), bat4,
                        prev8, ssm_prev.astype(F32), conv_w, arow, drow, acol, dcol, gnw, C,
                        _pick_tile(T // C, (4, 2, 1)))
    conv_new = conv_in.reshape(B, T, CONV_CH)[:, T - (CONV_W - 1):, :]

    if k_cache is None:
        tq = _pick_tile(T, (1024, 512, 256, 128))
        o_b = _attn_prompt(lam, qb, kbh, vbh, sw, tq, lam_init)
    else:
        o_b = _attn_sample(lam, qb.reshape(B, T, W_B), kbh.reshape(B, T, W_B), vbh.reshape(B, T, W_B),
                           k_cache, v_cache, sw, lam_init)
        o_b = o_b.reshape(n_tok, W_B)

    x1 = _outproj(x2d, o_a.reshape(n_tok, W_A), o_b, w_out, tm)

    tt = _pick_tile(n_tok, (512, 256, 128))
    h2, i1, i2, gate = _peer_score(x1, n2, wq, kblk, tt)
    g = _peer_gate(i1, i2, gate, _pick_tile(n_tok, (128,)))
    y = _peer_expert(h2, ut, v, g, x1, nf, _pick_tile(n_tok, (1024, 512, 256, 128)), PEER_NA * N_KEYS)
    return (y.reshape(B, T, D_MODEL), kb.reshape(B, T, H_B, 2 * DH_B), vb.reshape(B, T, H_B, DV_B),
            conv_new, ssm_new)


def _prep_weights(norm1_w, w_in, conv_w, a_log, dt_bias, gdn_norm_w, subln_w, w_out,
                  norm2_w, peer_wq, peer_k1, peer_k2, peer_u, peer_v, norm_f_w):
    o_beta = CONV_CH + W_A
    w_main = jnp.concatenate([w_in[:, :o_beta], w_in[:, o_beta + 2 * H_A:]], axis=1).astype(BF16)
    w_gate = w_in[:, o_beta:o_beta + 2 * H_A]
    w_ba = jnp.pad(w_gate, ((0, 0), (0, LANES - 2 * H_A))).astype(BF16)
    w_bat = w_gate.T.astype(BF16)
    zeros4 = jnp.zeros((H_A,), F32)
    a8 = jnp.concatenate([zeros4, a_log.astype(F32)])
    d8 = jnp.concatenate([zeros4, dt_bias.astype(F32)])
    arow = jnp.pad(a8, (0, LANES - 2 * H_A)).reshape(1, LANES)
    drow = jnp.pad(d8, (0, LANES - 2 * H_A)).reshape(1, LANES)
    acol = a8.reshape(2 * H_A, 1)
    dcol = d8.reshape(2 * H_A, 1)
    halves = jnp.stack([peer_k1, peer_k2]).astype(F32)
    eye = jnp.eye(2, dtype=F32)
    kblk = jnp.einsum("gj,gnd->gnjd", eye, halves).reshape(2 * N_KEYS, 2 * PEER_DK_HALF).astype(BF16)
    return (norm1_w.reshape(1, D_MODEL), w_main, w_ba, w_bat, conv_w, arow, drow, acol, dcol,
            gdn_norm_w.reshape(1, DK_A), subln_w.reshape(1, DV_B), w_out.astype(BF16),
            norm2_w.reshape(1, D_MODEL), peer_wq.astype(BF16), kblk,
            jnp.swapaxes(peer_u, 1, 2).astype(BF16), peer_v.astype(BF16), norm_f_w.reshape(1, D_MODEL))


def kernel(x_prompt, x_sample, cache_k, cache_v, state_conv, state_gdn, norm1_w, w_in, conv_w, a_log, dt_bias, gdn_norm_w, lam_q1, lam_k1, lam_q2, lam_k2, subln_w, w_out, norm2_w, peer_wq, peer_k1, peer_k2, peer_u, peer_v, norm_f_w):
    depth = w_in.shape[0]
    assert depth == 1, "the final norm is fused into the (single) layer's last kernel"
    l = 0
    lam_init = 0.8 - 0.6 * math.exp(-0.3 * l)
    lam = (jnp.exp(jnp.sum(lam_q1[l].astype(F32) * lam_k1[l].astype(F32)))
           - jnp.exp(jnp.sum(lam_q2[l].astype(F32) * lam_k2[l].astype(F32))) + lam_init).reshape(1)
    layer = lambda a: a.reshape(a.shape[1:])
    wts = _prep_weights(layer(norm1_w), layer(w_in), layer(conv_w), layer(a_log), layer(dt_bias),
                        layer(gdn_norm_w), layer(subln_w), layer(w_out), layer(norm2_w), layer(peer_wq),
                        layer(peer_k1), layer(peer_k2), peer_u, peer_v, norm_f_w)
    bp = x_prompt.shape[0]
    conv0 = jnp.zeros((bp, CONV_W - 1, CONV_CH), F32)
    ssm0 = jnp.zeros((bp, H_A, DK_A, DK_A), F32)
    yp, k1, v1, c1, s1 = _layer(x_prompt, conv0, ssm0, None, None, lam, lam_init, wts)
    ys, k2, v2, c2, s2 = _layer(x_sample, layer(state_conv), layer(state_gdn), layer(cache_k), layer(cache_v),
                                lam, lam_init, wts)
    st = lambda a: a[None]
    return (yp, ys, st(k1), st(v1), st(c1), st(s1), st(k2), st(v2), st(c2), st(s2))
```

```python
import functools
import math

import jax
import jax.numpy as jnp
from jax import lax
from jax.experimental import pallas as pl
from jax.experimental.pallas import tpu as pltpu

F32 = jnp.float32
BF16 = jnp.bfloat16
HIGHEST = lax.Precision.HIGHEST

D_MODEL = 1024
CHUNK = 64
H_A = 4
DK_A = 128
CONV_W = 4
W_A = H_A * DK_A
CONV_CH = 3 * W_A
H_B = 4
DH_B = 64
DV_B = 128
W_B = H_B * DV_B
DIFF_EPS = 1e-5
N_KEYS = 128
N_EXPERTS = N_KEYS * N_KEYS
PEER_HEADS = 8
PEER_TOPK = 16
PEER_DK_HALF = 64
PEER_NA = 16
EPS = 1e-6
LANES = 128
SUBLANES = 8
VMEM_LIMIT = 56 * 1024 * 1024

NT_DIMS = (((1,), (1,)), ((), ()))
TN_DIMS = (((0,), (0,)), ((), ()))


def _sigmoid(x):
    return 1.0 / (1.0 + jnp.exp(-x))


def _softplus(x):
    return jnp.maximum(x, 0.0) + jnp.log(1.0 + jnp.exp(-jnp.abs(x)))


def _dot_hi(a, b):
    return jnp.dot(a, b, precision=HIGHEST, preferred_element_type=F32)


def _dot_bf(a, b, dims=(((1,), (0,)), ((), ()))):
    return lax.dot_general(a.astype(BF16), b.astype(BF16), dims, preferred_element_type=F32)


def _split_bf(a):
    hi = a.astype(BF16)
    return hi, (a - hi.astype(F32)).astype(BF16)


def _dot_split(a, b):
    d = lambda x, y: jnp.dot(x, y, preferred_element_type=F32)
    return d(a[0], b[0]) + (d(a[0], b[1]) + d(a[1], b[0]))


def _inproj_kernel(x_ref, n1_ref, w_ref, wba_ref, wbat_ref,
                   conv_ref, z_ref, qb_ref, kb_ref, vb_ref, kbh_ref, vbh_ref, ba_ref, bat_ref):
    x = x_ref[...]
    ms = jnp.mean(x * x, axis=-1, keepdims=True)
    h = (x * lax.rsqrt(ms + EPS) * n1_ref[...]).astype(BF16)

    def mm(c0, c1):
        return jnp.dot(h, w_ref[:, c0:c1], preferred_element_type=F32)

    conv_ref[...] = mm(0, CONV_CH)
    z_ref[...] = mm(CONV_CH, CONV_CH + W_A)
    o = CONV_CH + W_A
    qb_ref[...] = (mm(o, o + W_B) * (DH_B ** -0.5)).astype(BF16)
    kb = mm(o + W_B, o + 2 * W_B)
    kb_ref[...] = kb
    kbh_ref[...] = kb.astype(BF16)
    vb = mm(o + 2 * W_B, o + 3 * W_B)
    vb_ref[...] = vb
    vbh_ref[...] = vb.astype(BF16)
    ba_ref[...] = jnp.dot(h, wba_ref[...], preferred_element_type=F32)
    bat_ref[...] = lax.dot_general(wbat_ref[...], h, NT_DIMS, preferred_element_type=F32)


def _inproj(x2d, n1, w_main, w_ba, w_bat, tm):
    T = x2d.shape[0]
    row = lambda c: pl.BlockSpec((tm, c), lambda i: (i, 0))
    full = lambda a: pl.BlockSpec(a.shape, lambda i: (0, 0))
    out_shape = (
        jax.ShapeDtypeStruct((T, CONV_CH), F32),
        jax.ShapeDtypeStruct((T, W_A), F32),
        jax.ShapeDtypeStruct((T, W_B), BF16),
        jax.ShapeDtypeStruct((T, W_B), F32),
        jax.ShapeDtypeStruct((T, W_B), F32),
        jax.ShapeDtypeStruct((T, W_B), BF16),
        jax.ShapeDtypeStruct((T, W_B), BF16),
        jax.ShapeDtypeStruct((T, LANES), F32),
        jax.ShapeDtypeStruct((SUBLANES, T), F32),
    )
    out_specs = (row(CONV_CH), row(W_A), row(W_B), row(W_B), row(W_B), row(W_B), row(W_B), row(LANES),
                 pl.BlockSpec((SUBLANES, tm), lambda i: (0, i)))
    return pl.pallas_call(
        _inproj_kernel,
        out_shape=out_shape,
        grid=(T // tm,),
        in_specs=[row(D_MODEL), full(n1), full(w_main), full(w_ba), full(w_bat)],
        out_specs=out_specs,
        compiler_params=pltpu.CompilerParams(dimension_semantics=("arbitrary",), vmem_limit_bytes=VMEM_LIMIT),
        name="inproj",
    )(x2d, n1, w_main, w_ba, w_bat)


def _gdn_kernel(conv_ref, z_ref, ba_ref, bat_ref, prev_ref, s0_ref, cw_ref, arow_ref, drow_ref,
                acol_ref, dcol_ref, nw_ref, o_ref, sout_ref, up_ref, s_ref, *, C, nc):
    t = pl.program_id(1)
    tt = nc * C

    @pl.when(t == 0)
    def _():
        up_ref[0:SUBLANES, :] = prev_ref[...]
        s_ref[...] = s0_ref[...]

    up_ref[SUBLANES:SUBLANES + tt, :] = conv_ref[...]
    y = up_ref[pl.ds(SUBLANES - CONV_W + 1, tt), :] * cw_ref[0:1, :]
    for j in range(1, CONV_W):
        y = y + up_ref[pl.ds(SUBLANES - CONV_W + 1 + j, tt), :] * cw_ref[j:j + 1, :]
    qkv = y * _sigmoid(y)
    tail = up_ref[tt:tt + SUBLANES, :]
    up_ref[0:SUBLANES, :] = tail

    ri = lax.broadcasted_iota(jnp.int32, (C, C), 0)
    ci = lax.broadcasted_iota(jnp.int32, (C, C), 1)
    incl = ri >= ci
    strict = ri > ci
    tril = incl.astype(F32)
    triu = (ri <= ci).astype(F32)
    eye = (ri == ci).astype(F32)

    ba = ba_ref[...]
    beta_all = _sigmoid(ba)
    g_all = -jnp.exp(arow_ref[...]) * _softplus(ba + drow_ref[...])
    nw = nw_ref[...]

    probs = []
    for c in range(nc):
        rows = slice(c * C, (c + 1) * C)
        gc_all = _dot_hi(tril, g_all[rows])
        g_t = -jnp.exp(acol_ref[...]) * _softplus(bat_ref[c] + dcol_ref[...])
        gc_t = _dot_hi(g_t, triu)
        for h in range(H_A):
            gcc = gc_all[:, H_A + h:H_A + h + 1]
            gcr = gc_t[H_A + h:H_A + h + 1, :]
            beta = beta_all[rows, h:h + 1]
            decay = jnp.where(incl, jnp.exp(jnp.where(incl, gcc - gcr, 0.0)), 0.0)
            qh = qkv[rows, h * DK_A:(h + 1) * DK_A]
            kh = qkv[rows, W_A + h * DK_A:W_A + (h + 1) * DK_A]
            vh = qkv[rows, 2 * W_A + h * DK_A:2 * W_A + (h + 1) * DK_A]
            qh = qh * lax.rsqrt(jnp.sum(qh * qh, axis=-1, keepdims=True) + EPS) * (DK_A ** -0.5)
            kh = kh * lax.rsqrt(jnp.sum(kh * kh, axis=-1, keepdims=True) + EPS)
            kb = kh * beta
            n = jnp.where(strict, -(_dot_bf(kb, kh, NT_DIMS) * decay), 0.0)
            egc = jnp.exp(gcc)
            g_last = gcc[C - 1:C, :]
            probs.append(dict(h=h, rows=rows, n=n, vb=(vh * beta).astype(BF16),
                              kbe=(kb * egc).astype(BF16), qg=(qh * egc).astype(BF16),
                              attn=(_dot_bf(qh, kh, NT_DIMS) * decay).astype(BF16),
                              kd=(kh * jnp.exp(g_last - gcc)).astype(BF16), eg=jnp.exp(g_last)))

    tinv = [eye + pr["n"] for pr in probs]
    pw = [_split_bf(pr["n"]) for pr in probs]
    for _ in range(int(math.log2(C)) - 1):
        pw = [_split_bf(_dot_split(p, p)) for p in pw]
        tinv = [ti + _dot_split(_split_bf(ti), p) for ti, p in zip(tinv, pw)]
    for pr, ti in zip(probs, tinv):
        ti = ti.astype(BF16)
        pr["u"] = jnp.dot(ti, pr["vb"], preferred_element_type=F32)
        pr["w"] = jnp.dot(ti, pr["kbe"], preferred_element_type=F32).astype(BF16)

    state = [s_ref[h] for h in range(H_A)]
    for pr in probs:
        h, rows = pr["h"], pr["rows"]
        s = state[h]
        s_bf = s.astype(BF16)
        v_new = (pr["u"] - jnp.dot(pr["w"], s_bf, preferred_element_type=F32)).astype(BF16)
        o = (jnp.dot(pr["qg"], s_bf, preferred_element_type=F32)
             + jnp.dot(pr["attn"], v_new, preferred_element_type=F32))
        state[h] = s * pr["eg"] + lax.dot_general(pr["kd"], v_new, TN_DIMS, preferred_element_type=F32)
        on = o * lax.rsqrt(jnp.mean(o * o, axis=-1, keepdims=True) + EPS) * nw
        zz = z_ref[rows, h * DK_A:(h + 1) * DK_A]
        o_ref[rows, h * DK_A:(h + 1) * DK_A] = (on * (zz * _sigmoid(zz))).astype(o_ref.dtype)
    for h in range(H_A):
        s_ref[h] = state[h]

    @pl.when(t == pl.num_programs(1) - 1)
    def _():
        sout_ref[...] = s_ref[...]


def _gdn(conv_in, z, ba, bat, prev8, s0, conv_w, arow, drow, acol, dcol, nw, C, nc):
    B, T, _ = conv_in.shape
    tt = nc * C
    tok = lambda c: pl.BlockSpec((None, tt, c), lambda b, t: (b, t, 0))
    full2 = lambda a: pl.BlockSpec(a.shape, lambda b, t: (0, 0))
    return pl.pallas_call(
        functools.partial(_gdn_kernel, C=C, nc=nc),
        out_shape=(jax.ShapeDtypeStruct((B, T, W_A), BF16),
                   jax.ShapeDtypeStruct((B, H_A, DK_A, DK_A), F32)),
        grid=(B, T // tt),
        in_specs=[tok(CONV_CH), tok(W_A), tok(LANES),
                  pl.BlockSpec((None, nc, SUBLANES, C), lambda b, t: (b, t, 0, 0)),
                  pl.BlockSpec((None, SUBLANES, CONV_CH), lambda b, t: (b, 0, 0)),
                  pl.BlockSpec((None, H_A, DK_A, DK_A), lambda b, t: (b, 0, 0, 0)),
                  full2(conv_w), full2(arow), full2(drow), full2(acol), full2(dcol), full2(nw)],
        out_specs=(tok(W_A), pl.BlockSpec((None, H_A, DK_A, DK_A), lambda b, t: (b, 0, 0, 0))),
        scratch_shapes=[pltpu.VMEM((SUBLANES + tt, CONV_CH), F32), pltpu.VMEM((H_A, DK_A, DK_A), F32)],
        compiler_params=pltpu.CompilerParams(dimension_semantics=("arbitrary", "arbitrary"),
                                             vmem_limit_bytes=VMEM_LIMIT),
        name="gdn",
    )(conv_in, z, ba, bat, prev8, s0, conv_w, arow, drow, acol, dcol, nw)


def _split_maps(q):
    lane = lax.broadcasted_iota(jnp.int32, q.shape, 1)
    zero = jnp.zeros_like(q)
    return jnp.concatenate([jnp.where(lane < DH_B, q, zero), jnp.where(lane >= DH_B, q, zero)], axis=0)


def _subln(acc, l, lam, sw, tq, lam_init):
    o = acc[:tq] / l[:tq] - lam * (acc[tq:] / l[tq:])
    return o * lax.rsqrt(jnp.mean(o * o, axis=-1, keepdims=True) + DIFF_EPS) * sw * (1.0 - lam_init)


def _attn_prompt_kernel(lam_ref, q_ref, k_ref, v_ref, sw_ref, o_ref, qs_ref, s0_ref, s1_ref, m_ref,
                        acc_ref, *, tq, rc, lam_init):
    qi = pl.program_id(1)
    qs_ref[...] = _split_maps(q_ref[...])
    m_ref[...] = jnp.full(m_ref.shape, -jnp.inf, F32)
    acc_ref[...] = jnp.zeros(acc_ref.shape, F32)
    chunks = [slice(c * rc, (c + 1) * rc) for c in range(2 * tq // rc)]
    rep = tq // LANES

    def scores(j, s_ref):
        kblk = k_ref[pl.ds(pl.multiple_of(j * tq, tq), tq), :]
        for rows in chunks:
            s_ref[rows, :] = lax.dot_general(qs_ref[rows, :], kblk, NT_DIMS, preferred_element_type=F32)

    def softmax_pv(j, s_ref, masked):
        vblk = v_ref[pl.ds(pl.multiple_of(j * tq, tq), tq), :]
        vext = jnp.concatenate([vblk, jnp.ones_like(vblk)], axis=1)
        for rows in chunks:
            s = s_ref[rows, :]
            if masked:
                r = lax.broadcasted_iota(jnp.int32, s.shape, 0) + (rows.start % tq)
                c = lax.broadcasted_iota(jnp.int32, s.shape, 1)
                s = jnp.where((c // CHUNK) <= (r // CHUNK), s, -jnp.inf)
            m_old = m_ref[rows, :]
            m_new = jnp.maximum(m_old, jnp.max(s, axis=-1, keepdims=True))
            alpha = jnp.exp(m_old - m_new)
            p = jnp.exp(s - jnp.concatenate([m_new] * rep, axis=1))
            acc_ref[rows, :] = (jnp.concatenate([alpha, alpha], axis=1) * acc_ref[rows, :]
                                + jnp.dot(p.astype(BF16), vext, preferred_element_type=F32))
            m_ref[rows, :] = m_new

    scores(qi, s0_ref)
    softmax_pv(qi, s0_ref, True)

    @pl.when(qi > 0)
    def _():
        scores(0, s0_ref)

        def pair(i, carry):
            j = 2 * i
            scores(j + 1, s1_ref)
            softmax_pv(j, s0_ref, False)
            scores(j + 2, s0_ref)
            softmax_pv(j + 1, s1_ref, False)
            return carry

        lax.fori_loop(0, qi // 2, pair, 0)

        @pl.when(qi % 2 == 1)
        def _():
            softmax_pv(qi - 1, s0_ref, False)

    o_ref[...] = _subln(acc_ref[:, 0:DV_B], acc_ref[:, DV_B:2 * DV_B], lam_ref[0], sw_ref[...], tq,
                        lam_init).astype(o_ref.dtype)


def _attn_prompt(lam, q, k, v, sw, tq, lam_init):
    T = q.shape[0]
    return pl.pallas_call(
        functools.partial(_attn_prompt_kernel, tq=tq, rc=min(512, tq), lam_init=lam_init),
        out_shape=jax.ShapeDtypeStruct((T, W_B), BF16),
        grid=(H_B, T // tq),
        in_specs=[pl.BlockSpec(memory_space=pltpu.SMEM),
                  pl.BlockSpec((tq, DV_B), lambda h, i: (i, h)),
                  pl.BlockSpec((T, DV_B), lambda h, i: (0, h)),
                  pl.BlockSpec((T, DV_B), lambda h, i: (0, h)),
                  pl.BlockSpec((1, DV_B), lambda h, i: (0, 0))],
        out_specs=pl.BlockSpec((tq, DV_B), lambda h, i: (i, h)),
        scratch_shapes=[pltpu.VMEM((2 * tq, DV_B), BF16), pltpu.VMEM((2 * tq, tq), F32),
                        pltpu.VMEM((2 * tq, tq), F32),
                        pltpu.VMEM((2 * tq, LANES), F32), pltpu.VMEM((2 * tq, 2 * DV_B), F32)],
        compiler_params=pltpu.CompilerParams(dimension_semantics=("arbitrary", "arbitrary"),
                                             vmem_limit_bytes=VMEM_LIMIT),
        name="attn_prompt",
    )(lam, q, k, v, sw)


def _attn_sample_kernel(lam_ref, q_ref, kn_ref, vn_ref, ck_ref, cv_ref, sw_ref, o_ref, *, tq, lam_init):
    for h in range(H_B):
        cols = slice(h * DV_B, (h + 1) * DV_B)
        qs = _split_maps(q_ref[:, cols])
        kc = ck_ref[:, h, :].astype(BF16)
        vc = cv_ref[:, h, :].astype(BF16)
        sc = lax.dot_general(qs, kc, NT_DIMS, preferred_element_type=F32)
        sn = lax.dot_general(qs, kn_ref[:, cols], NT_DIMS, preferred_element_type=F32)
        m = jnp.maximum(jnp.max(sc, axis=-1, keepdims=True), jnp.max(sn, axis=-1, keepdims=True))
        pc = jnp.exp(sc - m)
        pn = jnp.exp(sn - m)
        l = jnp.sum(pc, axis=-1, keepdims=True) + jnp.sum(pn, axis=-1, keepdims=True)
        acc = (jnp.dot(pc.astype(BF16), vc, preferred_element_type=F32)
               + jnp.dot(pn.astype(BF16), vn_ref[:, cols], preferred_element_type=F32))
        o_ref[:, cols] = _subln(acc, l, lam_ref[0], sw_ref[...], tq, lam_init).astype(o_ref.dtype)


def _attn_sample(lam, q, kn, vn, ck, cv, sw, lam_init):
    B, tq, _ = q.shape
    P = ck.shape[1]
    new = pl.BlockSpec((None, tq, W_B), lambda b: (b, 0, 0))
    cache = pl.BlockSpec((None, P, H_B, DV_B), lambda b: (b, 0, 0, 0))
    return pl.pallas_call(
        functools.partial(_attn_sample_kernel, tq=tq, lam_init=lam_init),
        out_shape=jax.ShapeDtypeStruct((B, tq, W_B), BF16),
        grid=(B,),
        in_specs=[pl.BlockSpec(memory_space=pltpu.SMEM), new, new, new, cache, cache,
                  pl.BlockSpec((1, DV_B), lambda b: (0, 0))],
        out_specs=new,
        compiler_params=pltpu.CompilerParams(dimension_semantics=("arbitrary",), vmem_limit_bytes=VMEM_LIMIT),
        name="attn_sample",
    )(lam, q, kn, vn, ck, cv, sw)


def _outproj_kernel(x_ref, oa_ref, ob_ref, w_ref, y_ref):
    y_ref[...] = (x_ref[...]
                  + jnp.dot(oa_ref[...], w_ref[0:W_A, :], preferred_element_type=F32)
                  + jnp.dot(ob_ref[...], w_ref[W_A:W_A + W_B, :], preferred_element_type=F32))


def _outproj(x2d, oa, ob, w_out, tm):
    T = x2d.shape[0]
    row = lambda c: pl.BlockSpec((tm, c), lambda i: (i, 0))
    return pl.pallas_call(
        _outproj_kernel,
        out_shape=jax.ShapeDtypeStruct((T, D_MODEL), F32),
        grid=(T // tm,),
        in_specs=[row(D_MODEL), row(W_A), row(W_B), pl.BlockSpec(w_out.shape, lambda i: (0, 0))],
        out_specs=row(D_MODEL),
        compiler_params=pltpu.CompilerParams(dimension_semantics=("arbitrary",), vmem_limit_bytes=VMEM_LIMIT),
        name="outproj",
    )(x2d, oa, ob, w_out)


_NEG_INF = float("-inf")
_PAIR_SLAB = SUBLANES
PEER_SLOTS = PEER_HEADS * PEER_TOPK


def _top16(s):
    n = s.shape[0]
    iota = lax.broadcasted_iota(jnp.int32, s.shape, 0)
    vals, idxs = [], []
    for _ in range(PEER_TOPK):
        m = jnp.max(s, axis=0, keepdims=True)
        first = jnp.min(jnp.where(s == m, iota, n), axis=0, keepdims=True)
        s = jnp.where(iota == first, _NEG_INF, s)
        vals.append(m)
        idxs.append(first.astype(F32))
    return vals, idxs


def _batcher_pairs(lo, hi):
    def merge(lo, hi, r):
        step = 2 * r
        if step < hi - lo:
            yield from merge(lo, hi, step)
            yield from merge(lo + r, hi, step)
            yield from ((i, i + r) for i in range(lo + r, hi - r, step))
        else:
            yield (lo, lo + r)

    if hi - lo >= 1:
        mid = lo + (hi - lo) // 2
        yield from _batcher_pairs(lo, mid)
        yield from _batcher_pairs(mid + 1, hi)
        yield from merge(lo, hi, 1)


def _exchange(v, ix, i, j):
    keep = v[i] >= v[j]
    v[i], v[j] = jnp.maximum(v[i], v[j]), jnp.minimum(v[i], v[j])
    ix[i], ix[j] = jnp.where(keep, ix[i], ix[j]), jnp.where(keep, ix[j], ix[i])


def _top16_sorted(s):
    n, t = s.shape
    groups = n // SUBLANES
    sub = lax.broadcasted_iota(jnp.int32, (SUBLANES, t), 0).astype(F32)
    v = [s[g * SUBLANES:(g + 1) * SUBLANES] for g in range(groups)]
    ix = [sub + float(g * SUBLANES) for g in range(groups)]
    for i, j in _batcher_pairs(0, groups - 1):
        _exchange(v, ix, i, j)
    for shift in (SUBLANES // 2, SUBLANES // 4, 1):
        pv = [pltpu.roll(x, shift, 0) for x in v]
        pi = [pltpu.roll(x, shift, 0) for x in ix]
        for i in range(groups):
            j = groups - 1 - i
            keep = v[i] >= pv[j]
            v[i] = jnp.maximum(v[i], pv[j])
            ix[i] = jnp.where(keep, ix[i], pi[j])
        stride = groups // 2
        while stride >= 1:
            for i in range(groups):
                if i & stride == 0:
                    _exchange(v, ix, i, i + stride)
            stride //= 2
    tied = jnp.zeros((1, t), F32)
    for p in range(groups - 1):
        tied = jnp.maximum(tied, jnp.where(v[p][0:1] == v[p + 1][0:1], 1.0, 0.0))
    at_least = jnp.sum(jnp.where(s >= v[groups - 1][0:1], 1.0, 0.0), axis=0, keepdims=True)
    tied = jnp.maximum(tied, jnp.where(at_least != float(PEER_TOPK), 1.0, 0.0))
    return [x[0:1] for x in v], [x[0:1] for x in ix], tied


def _top16_pairs(v1, v2):
    t = v1[0].shape[1]
    v1m = jnp.concatenate(v1, axis=0)
    v2m = jnp.concatenate(v2, axis=0)
    v2s = v2m[0:_PAIR_SLAB]
    i16 = lax.broadcasted_iota(jnp.int32, (PEER_TOPK, t), 0)
    i8 = lax.broadcasted_iota(jnp.int32, (_PAIR_SLAB, t), 0)
    slabs = [v1[0] + v2m, v1[1] + v2s]
    flats = [i16, PEER_TOPK + i8]
    for p in range(2, _PAIR_SLAB):
        slabs.append(jnp.where(i8 < PEER_TOPK // (p + 1), v1[p] + v2s, _NEG_INF))
        flats.append(PEER_TOPK * p + i8)
    slabs.append(v1m[_PAIR_SLAB:] + v2[0])
    flats.append(PEER_TOPK * (_PAIR_SLAB + i8))
    c = jnp.concatenate(slabs, axis=0)
    flat = jnp.concatenate(flats, axis=0)
    big = PEER_TOPK * PEER_TOPK
    sums, picks = [], []
    for _ in range(PEER_TOPK):
        m = jnp.max(c, axis=0, keepdims=True)
        first = jnp.min(jnp.where(c == m, flat, big), axis=0, keepdims=True)
        c = jnp.where(flat == first, _NEG_INF, c)
        sums.append(m)
        picks.append(first)
    pick = jnp.concatenate(picks, axis=0)
    p_sel = jnp.right_shift(pick, 4).astype(F32)
    q_sel = jnp.bitwise_and(pick, PEER_TOPK - 1).astype(F32)
    return jnp.concatenate(sums, axis=0), p_sel, q_sel


def _lookup(ranks, rows):
    out = jnp.zeros(ranks.shape, F32)
    for p in range(PEER_TOPK):
        out = jnp.where(ranks == float(p), rows[p], out)
    return out


def _peer_score_kernel(x_ref, n2_ref, wq_ref, kblk_ref, h_ref, i1_ref, i2_ref, w_ref,
                       st_ref, a_ref, b_ref, g_ref, *, tt):
    x = x_ref[...]
    ms = jnp.mean(x * x, axis=-1, keepdims=True)
    h = (x * lax.rsqrt(ms + EPS) * n2_ref[...]).astype(BF16)
    h_ref[...] = h
    q = jnp.dot(h, wq_ref[...], preferred_element_type=F32)
    qb = q.astype(BF16)
    for hd in range(PEER_HEADS):
        st_ref[hd * 2 * N_KEYS:(hd + 1) * 2 * N_KEYS, :] = lax.dot_general(
            kblk_ref[...], qb[:, hd * 2 * PEER_DK_HALF:(hd + 1) * 2 * PEER_DK_HALF], NT_DIMS,
            preferred_element_type=F32)

    def retrieve(hd, exact):
        r1 = pl.multiple_of(hd * (2 * N_KEYS), 2 * N_KEYS)
        slot = pl.multiple_of(hd * PEER_TOPK, PEER_TOPK)
        tied = jnp.zeros((1, LANES), F32)
        for g in range(tt // LANES):
            cols = slice(g * LANES, (g + 1) * LANES)
            s1 = st_ref[pl.ds(r1, N_KEYS), cols]
            s2 = st_ref[pl.ds(r1 + N_KEYS, N_KEYS), cols]
            if exact:
                (v1, i1), (v2, i2) = _top16(s1), _top16(s2)
            else:
                v1, i1, t1 = _top16_sorted(s1)
                v2, i2, t2 = _top16_sorted(s2)
                tied = jnp.maximum(tied, jnp.maximum(t1, t2))
            sums, p_sel, q_sel = _top16_pairs(v1, v2)
            e = jnp.exp(sums - sums[0:1])
            a_ref[pl.ds(slot, PEER_TOPK), cols] = _lookup(p_sel, i1)
            b_ref[pl.ds(slot, PEER_TOPK), cols] = _lookup(q_sel, i2)
            g_ref[pl.ds(slot, PEER_TOPK), cols] = e / jnp.sum(e, axis=0, keepdims=True)
        return tied

    def head(hd, carry):
        tied = retrieve(hd, False)

        @pl.when(jnp.max(tied) > 0.0)
        def _():
            retrieve(hd, True)

        return carry

    lax.fori_loop(0, PEER_HEADS, head, 0)

    i1_ref[...] = a_ref[...].T
    i2_ref[...] = b_ref[...].T
    w_ref[...] = g_ref[...].T


def _peer_score(x1, n2, wq, kblk, tt):
    T = x1.shape[0]
    slots = jax.ShapeDtypeStruct((T, PEER_SLOTS), F32)
    sspec = pl.BlockSpec((tt, PEER_SLOTS), lambda i: (i, 0))
    return pl.pallas_call(
        functools.partial(_peer_score_kernel, tt=tt),
        out_shape=(jax.ShapeDtypeStruct((T, D_MODEL), BF16), slots, slots, slots),
        grid=(T // tt,),
        in_specs=[pl.BlockSpec((tt, D_MODEL), lambda i: (i, 0)),
                  pl.BlockSpec(n2.shape, lambda i: (0, 0)),
                  pl.BlockSpec(wq.shape, lambda i: (0, 0)),
                  pl.BlockSpec(kblk.shape, lambda i: (0, 0))],
        out_specs=(pl.BlockSpec((tt, D_MODEL), lambda i: (i, 0)), sspec, sspec, sspec),
        scratch_shapes=[pltpu.VMEM((2 * PEER_HEADS * N_KEYS, tt), F32)]
                       + [pltpu.VMEM((PEER_SLOTS, tt), F32)] * 3,
        compiler_params=pltpu.CompilerParams(dimension_semantics=("arbitrary",), vmem_limit_bytes=VMEM_LIMIT),
        name="peer_score",
    )(x1, n2, wq, kblk)


_GATE_GROUP = 2 * SUBLANES


def _peer_gate_kernel(i1_ref, i2_ref, w_ref, g_ref, *, tg):
    key = lax.broadcasted_iota(jnp.int32, (N_KEYS, PEER_SLOTS), 0).astype(F32)

    def group(gi, carry):
        t0 = pl.multiple_of(gi * _GATE_GROUP, _GATE_GROUP)
        per_token = []
        for s in range(_GATE_GROUP):
            i1 = i1_ref[pl.ds(t0 + s, 1), :]
            i2 = i2_ref[pl.ds(t0 + s, 1), :]
            w = w_ref[pl.ds(t0 + s, 1), :]
            lhs = jnp.where(key == i1, w, 0.0).astype(BF16)
            rhs = jnp.where(key == i2, 1.0, 0.0).astype(BF16)
            per_token.append(lax.dot_general(lhs, rhs, NT_DIMS, preferred_element_type=F32))
        by_key = jnp.swapaxes(jnp.stack(per_token, axis=0), 0, 1)
        for a in range(N_KEYS):
            g_ref[pl.ds(t0, _GATE_GROUP), a * N_KEYS:(a + 1) * N_KEYS] = by_key[a].astype(BF16)
        return carry

    lax.fori_loop(0, tg // _GATE_GROUP, group, 0)


def _peer_gate(i1, i2, w, tg):
    T = i1.shape[0]
    sspec = pl.BlockSpec((tg, PEER_SLOTS), lambda i: (i, 0))
    return pl.pallas_call(
        functools.partial(_peer_gate_kernel, tg=tg),
        out_shape=jax.ShapeDtypeStruct((T, N_EXPERTS), BF16),
        grid=(T // tg,),
        in_specs=[sspec, sspec, sspec],
        out_specs=pl.BlockSpec((tg, N_EXPERTS), lambda i: (i, 0)),
        compiler_params=pltpu.CompilerParams(dimension_semantics=("arbitrary",), vmem_limit_bytes=VMEM_LIMIT),
        name="peer_gate",
    )(i1, i2, w)


def _peer_expert_kernel(h_ref, ut_ref, v_ref, g_ref, x_ref, nf_ref, y_ref, acc_ref):
    e = pl.program_id(1)

    @pl.when(e == 0)
    def _():
        acc_ref[...] = jnp.zeros(acc_ref.shape, F32)

    x = jnp.dot(h_ref[...], ut_ref[...], preferred_element_type=F32)
    act = 0.5 * x * (1.0 + lax.erf(x * (2.0 ** -0.5)))
    ga = (g_ref[...].astype(F32) * act).astype(BF16)
    acc_ref[...] += jnp.dot(ga, v_ref[...], preferred_element_type=F32)

    @pl.when(e == pl.num_programs(1) - 1)
    def _():
        xo = x_ref[...] + acc_ref[...]
        ms = jnp.mean(xo * xo, axis=-1, keepdims=True)
        y_ref[...] = xo * lax.rsqrt(ms + EPS) * nf_ref[...]


def _peer_expert(h, ut, v, g, x1, nf, tt, et):
    T = x1.shape[0]
    return pl.pallas_call(
        _peer_expert_kernel,
        out_shape=jax.ShapeDtypeStruct((T, D_MODEL), F32),
        grid=(T // tt, N_EXPERTS // et),
        in_specs=[pl.BlockSpec((tt, D_MODEL), lambda i, e: (i, 0)),
                  pl.BlockSpec((None, D_MODEL, et), lambda i, e: (0, 0, e)),
                  pl.BlockSpec((None, et, D_MODEL), lambda i, e: (0, e, 0)),
                  pl.BlockSpec((tt, et), lambda i, e: (i, e)),
                  pl.BlockSpec((tt, D_MODEL), lambda i, e: (i, 0), pipeline_mode=pl.Buffered(1)),
                  pl.BlockSpec(nf.shape, lambda i, e: (0, 0))],
        out_specs=pl.BlockSpec((tt, D_MODEL), lambda i, e: (i, 0)),
        scratch_shapes=[pltpu.VMEM((tt, D_MODEL), F32)],
        compiler_params=pltpu.CompilerParams(dimension_semantics=("arbitrary", "arbitrary"),
                                             vmem_limit_bytes=VMEM_LIMIT),
        name="peer_expert",
    )(h, ut, v, g, x1, nf)


def _pick_tile(T, prefs):
    for t in prefs:
        if T % t == 0:
            return t
    raise ValueError(f"no tile for {T}")


def _layer(x, conv_prev, ssm_prev, k_cache, v_cache, lam, lam_init, wts):
    (n1, w_main, w_ba, w_bat, conv_w, arow, drow, acol, dcol, gnw, sw, w_out, n2, wq, kblk, ut, v, nf) = wts
    B, T, _ = x.shape
    n_tok = B * T
    x2d = x.reshape(n_tok, D_MODEL)
    tm = _pick_tile(n_tok, (256, 128))
    conv_in, z, qb, kb, vb, kbh, vbh, ba, bat = _inproj(x2d, n1, w_main, w_ba, w_bat, tm)

    C = min(CHUNK, T)
    prev8 = jnp.pad(conv_prev.astype(F32), ((0, 0), (SUBLANES - CONV_W + 1, 0), (0, 0)))
    bat4 = bat.reshape(SUBLANES, B, T // C, C).transpose(1, 2, 0, 3)
    o_a, ssm_new = _gdn(conv_in.reshape(B, T, CONV_CH), z.reshape(B, T, W_A), ba.reshape(B, T, LANES), bat4,
                        prev8, ssm_prev.astype(F32), conv_w, arow, drow, acol, dcol, gnw, C,
                        _pick_tile(T // C, (4, 2, 1)))
    conv_new = conv_in.reshape(B, T, CONV_CH)[:, T - (CONV_W - 1):, :]

    if k_cache is None:
        tq = _pick_tile(T, (1024, 512, 256, 128))
        o_b = _attn_prompt(lam, qb, kbh, vbh, sw, tq, lam_init)
    else:
        o_b = _attn_sample(lam, qb.reshape(B, T, W_B), kbh.reshape(B, T, W_B), vbh.reshape(B, T, W_B),
                           k_cache, v_cache, sw, lam_init)
        o_b = o_b.reshape(n_tok, W_B)

    x1 = _outproj(x2d, o_a.reshape(n_tok, W_A), o_b, w_out, tm)

    tt = _pick_tile(n_tok, (512, 256, 128))
    h2, i1, i2, gate = _peer_score(x1, n2, wq, kblk, tt)
    g = _peer_gate(i1, i2, gate, _pick_tile(n_tok, (128,)))
    y = _peer_expert(h2, ut, v, g, x1, nf, _pick_tile(n_tok, (1024, 512, 256, 128)), PEER_NA * N_KEYS)
    return (y.reshape(B, T, D_MODEL), kb.reshape(B, T, H_B, 2 * DH_B), vb.reshape(B, T, H_B, DV_B),
            conv_new, ssm_new)


def _prep_weights(norm1_w, w_in, conv_w, a_log, dt_bias, gdn_norm_w, subln_w, w_out,
                  norm2_w, peer_wq, peer_k1, peer_k2, peer_u, peer_v, norm_f_w):
    o_beta = CONV_CH + W_A
    w_main = jnp.concatenate([w_in[:, :o_beta], w_in[:, o_beta + 2 * H_A:]], axis=1).astype(BF16)
    w_gate = w_in[:, o_beta:o_beta + 2 * H_A]
    w_ba = jnp.pad(w_gate, ((0, 0), (0, LANES - 2 * H_A))).astype(BF16)
    w_bat = w_gate.T.astype(BF16)
    zeros4 = jnp.zeros((H_A,), F32)
    a8 = jnp.concatenate([zeros4, a_log.astype(F32)])
    d8 = jnp.concatenate([zeros4, dt_bias.astype(F32)])
    arow = jnp.pad(a8, (0, LANES - 2 * H_A)).reshape(1, LANES)
    drow = jnp.pad(d8, (0, LANES - 2 * H_A)).reshape(1, LANES)
    acol = a8.reshape(2 * H_A, 1)
    dcol = d8.reshape(2 * H_A, 1)
    halves = jnp.stack([peer_k1, peer_k2]).astype(F32)
    eye = jnp.eye(2, dtype=F32)
    kblk = jnp.einsum("gj,gnd->gnjd", eye, halves).reshape(2 * N_KEYS, 2 * PEER_DK_HALF).astype(BF16)
    return (norm1_w.reshape(1, D_MODEL), w_main, w_ba, w_bat, conv_w, arow, drow, acol, dcol,
            gdn_norm_w.reshape(1, DK_A), subln_w.reshape(1, DV_B), w_out.astype(BF16),
            norm2_w.reshape(1, D_MODEL), peer_wq.astype(BF16), kblk,
            jnp.swapaxes(peer_u, 1, 2).astype(BF16), peer_v.astype(BF16), norm_f_w.reshape(1, D_MODEL))


def kernel(x_prompt, x_sample, cache_k, cache_v, state_conv, state_gdn, norm1_w, w_in, conv_w, a_log, dt_bias, gdn_norm_w, lam_q1, lam_k1, lam_q2, lam_k2, subln_w, w_out, norm2_w, peer_wq, peer_k1, peer_k2, peer_u, peer_v, norm_f_w):
    depth = w_in.shape[0]
    assert depth == 1, "the final norm is fused into the (single) layer's last kernel"
    l = 0
    lam_init = 0.8 - 0.6 * math.exp(-0.3 * l)
    lam = (jnp.exp(jnp.sum(lam_q1[l].astype(F32) * lam_k1[l].astype(F32)))
           - jnp.exp(jnp.sum(lam_q2[l].astype(F32) * lam_k2[l].astype(F32))) + lam_init).reshape(1)
    layer = lambda a: a.reshape(a.shape[1:])
    wts = _prep_weights(layer(norm1_w), layer(w_in), layer(conv_w), layer(a_log), layer(dt_bias),
                        layer(gdn_norm_w), layer(subln_w), layer(w_out), layer(norm2_w), layer(peer_wq),
                        layer(peer_k1), layer(peer_k2), peer_u, peer_v, norm_f_w)
    bp = x_prompt.shape[0]
    conv0 = jnp.zeros((bp, CONV_W - 1, CONV_CH), F32)
    ssm0 = jnp.zeros((bp, H_A, DK_A, DK_A), F32)
    yp, k1, v1, c1, s1 = _layer(x_prompt, conv0, ssm0, None, None, lam, lam_init, wts)
    ys, k2, v2, c2, s2 = _layer(x_sample, layer(state_conv), layer(state_gdn), layer(cache_k), layer(cache_v),
                                lam, lam_init, wts)
    st = lambda a: a[None]
    return (yp, ys, st(k1), st(v1), st(c1), st(s1), st(k2), st(v2), st(c2), st(s2))
```

```python
import functools
import math

import jax
import jax.numpy as jnp
from jax import lax
from jax.experimental import pallas as pl
from jax.experimental.pallas import tpu as pltpu

F32 = jnp.float32
BF16 = jnp.bfloat16
HIGHEST = lax.Precision.HIGHEST

D_MODEL = 1024
CHUNK = 64
H_A = 4
DK_A = 128
CONV_W = 4
W_A = H_A * DK_A
CONV_CH = 3 * W_A
H_B = 4
DH_B = 64
DV_B = 128
W_B = H_B * DV_B
DIFF_EPS = 1e-5
N_KEYS = 128
N_EXPERTS = N_KEYS * N_KEYS
PEER_HEADS = 8
PEER_TOPK = 16
PEER_DK_HALF = 64
PEER_NA = 16
EPS = 1e-6
LANES = 128
SUBLANES = 8
VMEM_LIMIT = 56 * 1024 * 1024

NT_DIMS = (((1,), (1,)), ((), ()))
TN_DIMS = (((0,), (0,)), ((), ()))


def _sigmoid(x):
    return 1.0 / (1.0 + jnp.exp(-x))


def _softplus(x):
    return jnp.maximum(x, 0.0) + jnp.log(1.0 + jnp.exp(-jnp.abs(x)))


def _dot_hi(a, b):
    return jnp.dot(a, b, precision=HIGHEST, preferred_element_type=F32)


def _dot_bf(a, b, dims=(((1,), (0,)), ((), ()))):
    return lax.dot_general(a.astype(BF16), b.astype(BF16), dims, preferred_element_type=F32)


def _split_bf(a):
    hi = a.astype(BF16)
    return hi, (a - hi.astype(F32)).astype(BF16)


def _dot_split(a, b):
    d = lambda x, y: jnp.dot(x, y, preferred_element_type=F32)
    return d(a[0], b[0]) + (d(a[0], b[1]) + d(a[1], b[0]))


def _inproj_kernel(x_ref, n1_ref, w_ref, wba_ref, wbat_ref,
                   conv_ref, z_ref, qb_ref, kb_ref, vb_ref, kbh_ref, vbh_ref, ba_ref, bat_ref):
    x = x_ref[...]
    ms = jnp.mean(x * x, axis=-1, keepdims=True)
    h = (x * lax.rsqrt(ms + EPS) * n1_ref[...]).astype(BF16)

    def mm(c0, c1):
        return jnp.dot(h, w_ref[:, c0:c1], preferred_element_type=F32)

    conv_ref[...] = mm(0, CONV_CH)
    z_ref[...] = mm(CONV_CH, CONV_CH + W_A)
    o = CONV_CH + W_A
    qb_ref[...] = (mm(o, o + W_B) * (DH_B ** -0.5)).astype(BF16)
    kb = mm(o + W_B, o + 2 * W_B)
    kb_ref[...] = kb
    kbh_ref[...] = kb.astype(BF16)
    vb = mm(o + 2 * W_B, o + 3 * W_B)
    vb_ref[...] = vb
    vbh_ref[...] = vb.astype(BF16)
    ba_ref[...] = jnp.dot(h, wba_ref[...], preferred_element_type=F32)
    bat_ref[...] = lax.dot_general(wbat_ref[...], h, NT_DIMS, preferred_element_type=F32)


def _inproj(x2d, n1, w_main, w_ba, w_bat, tm):
    T = x2d.shape[0]
    row = lambda c: pl.BlockSpec((tm, c), lambda i: (i, 0))
    full = lambda a: pl.BlockSpec(a.shape, lambda i: (0, 0))
    out_shape = (
        jax.ShapeDtypeStruct((T, CONV_CH), F32),
        jax.ShapeDtypeStruct((T, W_A), F32),
        jax.ShapeDtypeStruct((T, W_B), BF16),
        jax.ShapeDtypeStruct((T, W_B), F32),
        jax.ShapeDtypeStruct((T, W_B), F32),
        jax.ShapeDtypeStruct((T, W_B), BF16),
        jax.ShapeDtypeStruct((T, W_B), BF16),
        jax.ShapeDtypeStruct((T, LANES), F32),
        jax.ShapeDtypeStruct((SUBLANES, T), F32),
    )
    out_specs = (row(CONV_CH), row(W_A), row(W_B), row(W_B), row(W_B), row(W_B), row(W_B), row(LANES),
                 pl.BlockSpec((SUBLANES, tm), lambda i: (0, i)))
    return pl.pallas_call(
        _inproj_kernel,
        out_shape=out_shape,
        grid=(T // tm,),
        in_specs=[row(D_MODEL), full(n1), full(w_main), full(w_ba), full(w_bat)],
        out_specs=out_specs,
        compiler_params=pltpu.CompilerParams(dimension_semantics=("arbitrary",), vmem_limit_bytes=VMEM_LIMIT),
        name="inproj",
    )(x2d, n1, w_main, w_ba, w_bat)


def _gdn_kernel(conv_ref, z_ref, ba_ref, bat_ref, prev_ref, s0_ref, cw_ref, arow_ref, drow_ref,
                acol_ref, dcol_ref, nw_ref, o_ref, sout_ref, up_ref, s_ref, *, C, nc):
    t = pl.program_id(1)
    tt = nc * C

    @pl.when(t == 0)
    def _():
        up_ref[0:SUBLANES, :] = prev_ref[...]
        s_ref[...] = s0_ref[...]

    up_ref[SUBLANES:SUBLANES + tt, :] = conv_ref[...]
    y = up_ref[pl.ds(SUBLANES - CONV_W + 1, tt), :] * cw_ref[0:1, :]
    for j in range(1, CONV_W):
        y = y + up_ref[pl.ds(SUBLANES - CONV_W + 1 + j, tt), :] * cw_ref[j:j + 1, :]
    qkv = y * _sigmoid(y)
    tail = up_ref[tt:tt + SUBLANES, :]
    up_ref[0:SUBLANES, :] = tail

    ri = lax.broadcasted_iota(jnp.int32, (C, C), 0)
    ci = lax.broadcasted_iota(jnp.int32, (C, C), 1)
    incl = ri >= ci
    strict = ri > ci
    tril = incl.astype(F32)
    triu = (ri <= ci).astype(F32)
    eye = (ri == ci).astype(F32)

    ba = ba_ref[...]
    beta_all = _sigmoid(ba)
    g_all = -jnp.exp(arow_ref[...]) * _softplus(ba + drow_ref[...])
    nw = nw_ref[...]

    probs = []
    for c in range(nc):
        rows = slice(c * C, (c + 1) * C)
        gc_all = _dot_hi(tril, g_all[rows])
        g_t = -jnp.exp(acol_ref[...]) * _softplus(bat_ref[c] + dcol_ref[...])
        gc_t = _dot_hi(g_t, triu)
        for h in range(H_A):
            gcc = gc_all[:, H_A + h:H_A + h + 1]
            gcr = gc_t[H_A + h:H_A + h + 1, :]
            beta = beta_all[rows, h:h + 1]
            decay = jnp.where(incl, jnp.exp(jnp.where(incl, gcc - gcr, 0.0)), 0.0)
            qh = qkv[rows, h * DK_A:(h + 1) * DK_A]
            kh = qkv[rows, W_A + h * DK_A:W_A + (h + 1) * DK_A]
            vh = qkv[rows, 2 * W_A + h * DK_A:2 * W_A + (h + 1) * DK_A]
            qh = qh * lax.rsqrt(jnp.sum(qh * qh, axis=-1, keepdims=True) + EPS) * (DK_A ** -0.5)
            kh = kh * lax.rsqrt(jnp.sum(kh * kh, axis=-1, keepdims=True) + EPS)
            kb = kh * beta
            n = jnp.where(strict, -(_dot_bf(kb, kh, NT_DIMS) * decay), 0.0)
            egc = jnp.exp(gcc)
            g_last = gcc[C - 1:C, :]
            probs.append(dict(h=h, rows=rows, n=n, vb=(vh * beta).astype(BF16),
                              kbe=(kb * egc).astype(BF16), qg=(qh * egc).astype(BF16),
                              attn=(_dot_bf(qh, kh, NT_DIMS) * decay).astype(BF16),
                              kd=(kh * jnp.exp(g_last - gcc)).astype(BF16), eg=jnp.exp(g_last)))

    tinv = [eye + pr["n"] for pr in probs]
    pw = [_split_bf(pr["n"]) for pr in probs]
    for _ in range(int(math.log2(C)) - 1):
        pw = [_split_bf(_dot_split(p, p)) for p in pw]
        tinv = [ti + _dot_split(_split_bf(ti), p) for ti, p in zip(tinv, pw)]
    for pr, ti in zip(probs, tinv):
        ti = ti.astype(BF16)
        pr["u"] = jnp.dot(ti, pr["vb"], preferred_element_type=F32)
        pr["w"] = jnp.dot(ti, pr["kbe"], preferred_element_type=F32).astype(BF16)

    state = [s_ref[h] for h in range(H_A)]
    for pr in probs:
        h, rows = pr["h"], pr["rows"]
        s = state[h]
        s_bf = s.astype(BF16)
        v_new = (pr["u"] - jnp.dot(pr["w"], s_bf, preferred_element_type=F32)).astype(BF16)
        o = (jnp.dot(pr["qg"], s_bf, preferred_element_type=F32)
             + jnp.dot(pr["attn"], v_new, preferred_element_type=F32))
        state[h] = s * pr["eg"] + lax.dot_general(pr["kd"], v_new, TN_DIMS, preferred_element_type=F32)
        on = o * lax.rsqrt(jnp.mean(o * o, axis=-1, keepdims=True) + EPS) * nw
        zz = z_ref[rows, h * DK_A:(h + 1) * DK_A]
        o_ref[rows, h * DK_A:(h + 1) * DK_A] = (on * (zz * _sigmoid(zz))).astype(o_ref.dtype)
    for h in range(H_A):
        s_ref[h] = state[h]

    @pl.when(t == pl.num_programs(1) - 1)
    def _():
        sout_ref[...] = s_ref[...]


def _gdn(conv_in, z, ba, bat, prev8, s0, conv_w, arow, drow, acol, dcol, nw, C, nc):
    B, T, _ = conv_in.shape
    tt = nc * C
    tok = lambda c: pl.BlockSpec((None, tt, c), lambda b, t: (b, t, 0))
    full2 = lambda a: pl.BlockSpec(a.shape, lambda b, t: (0, 0))
    return pl.pallas_call(
        functools.partial(_gdn_kernel, C=C, nc=nc),
        out_shape=(jax.ShapeDtypeStruct((B, T, W_A), BF16),
                   jax.ShapeDtypeStruct((B, H_A, DK_A, DK_A), F32)),
        grid=(B, T // tt),
        in_specs=[tok(CONV_CH), tok(W_A), tok(LANES),
                  pl.BlockSpec((None, nc, SUBLANES, C), lambda b, t: (b, t, 0, 0)),
                  pl.BlockSpec((None, SUBLANES, CONV_CH), lambda b, t: (b, 0, 0)),
                  pl.BlockSpec((None, H_A, DK_A, DK_A), lambda b, t: (b, 0, 0, 0)),
                  full2(conv_w), full2(arow), full2(drow), full2(acol), full2(dcol), full2(nw)],
        out_specs=(tok(W_A), pl.BlockSpec((None, H_A, DK_A, DK_A), lambda b, t: (b, 0, 0, 0))),
        scratch_shapes=[pltpu.VMEM((SUBLANES + tt, CONV_CH), F32), pltpu.VMEM((H_A, DK_A, DK_A), F32)],
        compiler_params=pltpu.CompilerParams(dimension_semantics=("arbitrary", "arbitrary"),
                                             vmem_limit_bytes=VMEM_LIMIT),
        name="gdn",
    )(conv_in, z, ba, bat, prev8, s0, conv_w, arow, drow, acol, dcol, nw)


def _split_maps(q):
    lane = lax.broadcasted_iota(jnp.int32, q.shape, 1)
    zero = jnp.zeros_like(q)
    return jnp.concatenate([jnp.where(lane < DH_B, q, zero), jnp.where(lane >= DH_B, q, zero)], axis=0)


def _subln(acc, l, lam, sw, tq, lam_init):
    o = acc[:tq] / l[:tq] - lam * (acc[tq:] / l[tq:])
    return o * lax.rsqrt(jnp.mean(o * o, axis=-1, keepdims=True) + DIFF_EPS) * sw * (1.0 - lam_init)


def _attn_prompt_kernel(lam_ref, q_ref, k_ref, v_ref, sw_ref, o_ref, qs_ref, s0_ref, s1_ref, m_ref,
                        acc_ref, *, tq, rc, lam_init):
    qi = pl.program_id(1)
    qs_ref[...] = _split_maps(q_ref[...])
    m_ref[...] = jnp.full(m_ref.shape, -jnp.inf, F32)
    acc_ref[...] = jnp.zeros(acc_ref.shape, F32)
    chunks = [slice(c * rc, (c + 1) * rc) for c in range(2 * tq // rc)]
    rep = tq // LANES

    def scores(j, s_ref):
        kblk = k_ref[pl.ds(pl.multiple_of(j * tq, tq), tq), :]
        for rows in chunks:
            s_ref[rows, :] = lax.dot_general(qs_ref[rows, :], kblk, NT_DIMS, preferred_element_type=F32)

    def softmax_pv(j, s_ref, masked):
        vblk = v_ref[pl.ds(pl.multiple_of(j * tq, tq), tq), :]
        vext = jnp.concatenate([vblk, jnp.ones_like(vblk)], axis=1)
        for rows in chunks:
            s = s_ref[rows, :]
            if masked:
                r = lax.broadcasted_iota(jnp.int32, s.shape, 0) + (rows.start % tq)
                c = lax.broadcasted_iota(jnp.int32, s.shape, 1)
                s = jnp.where((c // CHUNK) <= (r // CHUNK), s, -jnp.inf)
            m_old = m_ref[rows, :]
            m_new = jnp.maximum(m_old, jnp.max(s, axis=-1, keepdims=True))
            alpha = jnp.exp(m_old - m_new)
            p = jnp.exp(s - jnp.concatenate([m_new] * rep, axis=1))
            acc_ref[rows, :] = (jnp.concatenate([alpha, alpha], axis=1) * acc_ref[rows, :]
                                + jnp.dot(p.astype(BF16), vext, preferred_element_type=F32))
            m_ref[rows, :] = m_new

    scores(qi, s0_ref)
    softmax_pv(qi, s0_ref, True)

    @pl.when(qi > 0)
    def _():
        scores(0, s0_ref)

        def pair(i, carry):
            j = 2 * i
            scores(j + 1, s1_ref)
            softmax_pv(j, s0_ref, False)
            scores(j + 2, s0_ref)
            softmax_pv(j + 1, s1_ref, False)
            return carry

        lax.fori_loop(0, qi // 2, pair, 0)

        @pl.when(qi % 2 == 1)
        def _():
            softmax_pv(qi - 1, s0_ref, False)

    o_ref[...] = _subln(acc_ref[:, 0:DV_B], acc_ref[:, DV_B:2 * DV_B], lam_ref[0], sw_ref[...], tq,
                        lam_init).astype(o_ref.dtype)


def _attn_prompt(lam, q, k, v, sw, tq, lam_init):
    T = q.shape[0]
    return pl.pallas_call(
        functools.partial(_attn_prompt_kernel, tq=tq, rc=min(512, tq), lam_init=lam_init),
        out_shape=jax.ShapeDtypeStruct((T, W_B), BF16),
        grid=(H_B, T // tq),
        in_specs=[pl.BlockSpec(memory_space=pltpu.SMEM),
                  pl.BlockSpec((tq, DV_B), lambda h, i: (i, h)),
                  pl.BlockSpec((T, DV_B), lambda h, i: (0, h)),
                  pl.BlockSpec((T, DV_B), lambda h, i: (0, h)),
                  pl.BlockSpec((1, DV_B), lambda h, i: (0, 0))],
        out_specs=pl.BlockSpec((tq, DV_B), lambda h, i: (i, h)),
        scratch_shapes=[pltpu.VMEM((2 * tq, DV_B), BF16), pltpu.VMEM((2 * tq, tq), F32),
                        pltpu.VMEM((2 * tq, tq), F32),
                        pltpu.VMEM((2 * tq, LANES), F32), pltpu.VMEM((2 * tq, 2 * DV_B), F32)],
        compiler_params=pltpu.CompilerParams(dimension_semantics=("arbitrary", "arbitrary"),
                                             vmem_limit_bytes=VMEM_LIMIT),
        name="attn_prompt",
    )(lam, q, k, v, sw)


def _attn_sample_kernel(lam_ref, q_ref, kn_ref, vn_ref, ck_ref, cv_ref, sw_ref, o_ref, *, tq, lam_init):
    for h in range(H_B):
        cols = slice(h * DV_B, (h + 1) * DV_B)
        qs = _split_maps(q_ref[:, cols])
        kc = ck_ref[:, h, :].astype(BF16)
        vc = cv_ref[:, h, :].astype(BF16)
        sc = lax.dot_general(qs, kc, NT_DIMS, preferred_element_type=F32)
        sn = lax.dot_general(qs, kn_ref[:, cols], NT_DIMS, preferred_element_type=F32)
        m = jnp.maximum(jnp.max(sc, axis=-1, keepdims=True), jnp.max(sn, axis=-1, keepdims=True))
        pc = jnp.exp(sc - m)
        pn = jnp.exp(sn - m)
        l = jnp.sum(pc, axis=-1, keepdims=True) + jnp.sum(pn, axis=-1, keepdims=True)
        acc = (jnp.dot(pc.astype(BF16), vc, preferred_element_type=F32)
               + jnp.dot(pn.astype(BF16), vn_ref[:, cols], preferred_element_type=F32))
        o_ref[:, cols] = _subln(acc, l, lam_ref[0], sw_ref[...], tq, lam_init).astype(o_ref.dtype)


def _attn_sample(lam, q, kn, vn, ck, cv, sw, lam_init):
    B, tq, _ = q.shape
    P = ck.shape[1]
    new = pl.BlockSpec((None, tq, W_B), lambda b: (b, 0, 0))
    cache = pl.BlockSpec((None, P, H_B, DV_B), lambda b: (b, 0, 0, 0))
    return pl.pallas_call(
        functools.partial(_attn_sample_kernel, tq=tq, lam_init=lam_init),
        out_shape=jax.ShapeDtypeStruct((B, tq, W_B), BF16),
        grid=(B,),
        in_specs=[pl.BlockSpec(memory_space=pltpu.SMEM), new, new, new, cache, cache,
                  pl.BlockSpec((1, DV_B), lambda b: (0, 0))],
        out_specs=new,
        compiler_params=pltpu.CompilerParams(dimension_semantics=("arbitrary",), vmem_limit_bytes=VMEM_LIMIT),
        name="attn_sample",
    )(lam, q, kn, vn, ck, cv, sw)


def _outproj_kernel(x_ref, oa_ref, ob_ref, w_ref, y_ref):
    y_ref[...] = (x_ref[...]
                  + jnp.dot(oa_ref[...], w_ref[0:W_A, :], preferred_element_type=F32)
                  + jnp.dot(ob_ref[...], w_ref[W_A:W_A + W_B, :], preferred_element_type=F32))


def _outproj(x2d, oa, ob, w_out, tm):
    T = x2d.shape[0]
    row = lambda c: pl.BlockSpec((tm, c), lambda i: (i, 0))
    return pl.pallas_call(
        _outproj_kernel,
        out_shape=jax.ShapeDtypeStruct((T, D_MODEL), F32),
        grid=(T // tm,),
        in_specs=[row(D_MODEL), row(W_A), row(W_B), pl.BlockSpec(w_out.shape, lambda i: (0, 0))],
        out_specs=row(D_MODEL),
        compiler_params=pltpu.CompilerParams(dimension_semantics=("arbitrary",), vmem_limit_bytes=VMEM_LIMIT),
        name="outproj",
    )(x2d, oa, ob, w_out)


_NEG_INF = float("-inf")
_PAIR_SLAB = SUBLANES
PEER_SLOTS = PEER_HEADS * PEER_TOPK


def _top16(s):
    n = s.shape[0]
    iota = lax.broadcasted_iota(jnp.int32, s.shape, 0)
    vals, idxs = [], []
    for _ in range(PEER_TOPK):
        m = jnp.max(s, axis=0, keepdims=True)
        first = jnp.min(jnp.where(s == m, iota, n), axis=0, keepdims=True)
        s = jnp.where(iota == first, _NEG_INF, s)
        vals.append(m)
        idxs.append(first.astype(F32))
    return vals, idxs


def _batcher_pairs(lo, hi):
    def merge(lo, hi, r):
        step = 2 * r
        if step < hi - lo:
            yield from merge(lo, hi, step)
            yield from merge(lo + r, hi, step)
            yield from ((i, i + r) for i in range(lo + r, hi - r, step))
        else:
            yield (lo, lo + r)

    if hi - lo >= 1:
        mid = lo + (hi - lo) // 2
        yield from _batcher_pairs(lo, mid)
        yield from _batcher_pairs(mid + 1, hi)
        yield from merge(lo, hi, 1)


def _exchange(v, ix, i, j):
    keep = v[i] >= v[j]
    v[i], v[j] = jnp.maximum(v[i], v[j]), jnp.minimum(v[i], v[j])
    ix[i], ix[j] = jnp.where(keep, ix[i], ix[j]), jnp.where(keep, ix[j], ix[i])


def _top16_sorted(s):
    n, t = s.shape
    groups = n // SUBLANES
    sub = lax.broadcasted_iota(jnp.int32, (SUBLANES, t), 0).astype(F32)
    v = [s[g * SUBLANES:(g + 1) * SUBLANES] for g in range(groups)]
    ix = [sub + float(g * SUBLANES) for g in range(groups)]
    for i, j in _batcher_pairs(0, groups - 1):
        _exchange(v, ix, i, j)
    for shift in (SUBLANES // 2, SUBLANES // 4, 1):
        pv = [pltpu.roll(x, shift, 0) for x in v]
        pi = [pltpu.roll(x, shift, 0) for x in ix]
        for i in range(groups):
            j = groups - 1 - i
            keep = v[i] >= pv[j]
            v[i] = jnp.maximum(v[i], pv[j])
            ix[i] = jnp.where(keep, ix[i], pi[j])
        stride = groups // 2
        while stride >= 1:
            for i in range(groups):
                if i & stride == 0:
                    _exchange(v, ix, i, i + stride)
            stride //= 2
    tied = jnp.zeros((1, t), F32)
    for p in range(groups - 1):
        tied = jnp.maximum(tied, jnp.where(v[p][0:1] == v[p + 1][0:1], 1.0, 0.0))
    at_least = jnp.sum(jnp.where(s >= v[groups - 1][0:1], 1.0, 0.0), axis=0, keepdims=True)
    tied = jnp.maximum(tied, jnp.where(at_least != float(PEER_TOPK), 1.0, 0.0))
    return [x[0:1] for x in v], [x[0:1] for x in ix], tied


def _top16_pairs(v1, v2):
    t = v1[0].shape[1]
    v1m = jnp.concatenate(v1, axis=0)
    v2m = jnp.concatenate(v2, axis=0)
    v2s = v2m[0:_PAIR_SLAB]
    i16 = lax.broadcasted_iota(jnp.int32, (PEER_TOPK, t), 0)
    i8 = lax.broadcasted_iota(jnp.int32, (_PAIR_SLAB, t), 0)
    slabs = [v1[0] + v2m, v1[1] + v2s]
    flats = [i16, PEER_TOPK + i8]
    for p in range(2, _PAIR_SLAB):
        slabs.append(jnp.where(i8 < PEER_TOPK // (p + 1), v1[p] + v2s, _NEG_INF))
        flats.append(PEER_TOPK * p + i8)
    slabs.append(v1m[_PAIR_SLAB:] + v2[0])
    flats.append(PEER_TOPK * (_PAIR_SLAB + i8))
    c = jnp.concatenate(slabs, axis=0)
    flat = jnp.concatenate(flats, axis=0)
    big = PEER_TOPK * PEER_TOPK
    sums, picks = [], []
    for _ in range(PEER_TOPK):
        m = jnp.max(c, axis=0, keepdims=True)
        first = jnp.min(jnp.where(c == m, flat, big), axis=0, keepdims=True)
        c = jnp.where(flat == first, _NEG_INF, c)
        sums.append(m)
        picks.append(first)
    pick = jnp.concatenate(picks, axis=0)
    p_sel = jnp.right_shift(pick, 4).astype(F32)
    q_sel = jnp.bitwise_and(pick, PEER_TOPK - 1).astype(F32)
    return jnp.concatenate(sums, axis=0), p_sel, q_sel


def _lookup(ranks, rows):
    out = jnp.zeros(ranks.shape, F32)
    for p in range(PEER_TOPK):
        out = jnp.where(ranks == float(p), rows[p], out)
    return out


def _peer_score_kernel(xin_ref, oa_ref, ob_ref, wo_ref, n2_ref, wq_ref, kblk_ref, x_ref, h_ref, i1_ref, i2_ref,
                       w_ref, st_ref, a_ref, b_ref, g_ref, *, tt):
    x = (xin_ref[...]
         + jnp.dot(oa_ref[...], wo_ref[0:W_A, :], preferred_element_type=F32)
         + jnp.dot(ob_ref[...], wo_ref[W_A:W_A + W_B, :], preferred_element_type=F32))
    x_ref[...] = x
    ms = jnp.mean(x * x, axis=-1, keepdims=True)
    h = (x * lax.rsqrt(ms + EPS) * n2_ref[...]).astype(BF16)
    h_ref[...] = h
    q = jnp.dot(h, wq_ref[...], preferred_element_type=F32)
    qb = q.astype(BF16)
    for hd in range(PEER_HEADS):
        st_ref[hd * 2 * N_KEYS:(hd + 1) * 2 * N_KEYS, :] = lax.dot_general(
            kblk_ref[...], qb[:, hd * 2 * PEER_DK_HALF:(hd + 1) * 2 * PEER_DK_HALF], NT_DIMS,
            preferred_element_type=F32)

    def retrieve(hd, exact):
        r1 = pl.multiple_of(hd * (2 * N_KEYS), 2 * N_KEYS)
        slot = pl.multiple_of(hd * PEER_TOPK, PEER_TOPK)
        tied = jnp.zeros((1, LANES), F32)
        for g in range(tt // LANES):
            cols = slice(g * LANES, (g + 1) * LANES)
            s1 = st_ref[pl.ds(r1, N_KEYS), cols]
            s2 = st_ref[pl.ds(r1 + N_KEYS, N_KEYS), cols]
            if exact:
                (v1, i1), (v2, i2) = _top16(s1), _top16(s2)
            else:
                v1, i1, t1 = _top16_sorted(s1)
                v2, i2, t2 = _top16_sorted(s2)
                tied = jnp.maximum(tied, jnp.maximum(t1, t2))
            sums, p_sel, q_sel = _top16_pairs(v1, v2)
            e = jnp.exp(sums - sums[0:1])
            a_ref[pl.ds(slot, PEER_TOPK), cols] = _lookup(p_sel, i1)
            b_ref[pl.ds(slot, PEER_TOPK), cols] = _lookup(q_sel, i2)
            g_ref[pl.ds(slot, PEER_TOPK), cols] = e / jnp.sum(e, axis=0, keepdims=True)
        return tied

    def head(hd, carry):
        tied = retrieve(hd, False)

        @pl.when(jnp.max(tied) > 0.0)
        def _():
            retrieve(hd, True)

        return carry

    lax.fori_loop(0, PEER_HEADS, head, 0)

    i1_ref[...] = a_ref[...].T
    i2_ref[...] = b_ref[...].T
    w_ref[...] = g_ref[...].T


def _peer_score(x2d, oa, ob, w_out, n2, wq, kblk, tt):
    T = x2d.shape[0]
    slots = jax.ShapeDtypeStruct((T, PEER_SLOTS), F32)
    sspec = pl.BlockSpec((tt, PEER_SLOTS), lambda i: (i, 0))
    row = lambda c: pl.BlockSpec((tt, c), lambda i: (i, 0))
    return pl.pallas_call(
        functools.partial(_peer_score_kernel, tt=tt),
        out_shape=(jax.ShapeDtypeStruct((T, D_MODEL), F32), jax.ShapeDtypeStruct((T, D_MODEL), BF16),
                   slots, slots, slots),
        grid=(T // tt,),
        in_specs=[row(D_MODEL), row(W_A), row(W_B),
                  pl.BlockSpec(w_out.shape, lambda i: (0, 0)),
                  pl.BlockSpec(n2.shape, lambda i: (0, 0)),
                  pl.BlockSpec(wq.shape, lambda i: (0, 0)),
                  pl.BlockSpec(kblk.shape, lambda i: (0, 0))],
        out_specs=(row(D_MODEL), row(D_MODEL), sspec, sspec, sspec),
        scratch_shapes=[pltpu.VMEM((2 * PEER_HEADS * N_KEYS, tt), F32)]
                       + [pltpu.VMEM((PEER_SLOTS, tt), F32)] * 3,
        compiler_params=pltpu.CompilerParams(dimension_semantics=("arbitrary",), vmem_limit_bytes=VMEM_LIMIT),
        name="peer_score",
    )(x2d, oa, ob, w_out, n2, wq, kblk)


_GATE_GROUP = 2 * SUBLANES


def _peer_gate_kernel(i1_ref, i2_ref, w_ref, g_ref, *, tg):
    key = lax.broadcasted_iota(jnp.int32, (N_KEYS, PEER_SLOTS), 0).astype(F32)

    def group(gi, carry):
        t0 = pl.multiple_of(gi * _GATE_GROUP, _GATE_GROUP)
        per_token = []
        for s in range(_GATE_GROUP):
            i1 = i1_ref[pl.ds(t0 + s, 1), :]
            i2 = i2_ref[pl.ds(t0 + s, 1), :]
            w = w_ref[pl.ds(t0 + s, 1), :]
            lhs = jnp.where(key == i1, w, 0.0).astype(BF16)
            rhs = jnp.where(key == i2, 1.0, 0.0).astype(BF16)
            per_token.append(lax.dot_general(lhs, rhs, NT_DIMS, preferred_element_type=F32))
        by_key = jnp.swapaxes(jnp.stack(per_token, axis=0), 0, 1)
        for a in range(N_KEYS):
            g_ref[pl.ds(t0, _GATE_GROUP), a * N_KEYS:(a + 1) * N_KEYS] = by_key[a].astype(BF16)
        return carry

    lax.fori_loop(0, tg // _GATE_GROUP, group, 0)


def _peer_gate(i1, i2, w, tg):
    T = i1.shape[0]
    sspec = pl.BlockSpec((tg, PEER_SLOTS), lambda i: (i, 0))
    return pl.pallas_call(
        functools.partial(_peer_gate_kernel, tg=tg),
        out_shape=jax.ShapeDtypeStruct((T, N_EXPERTS), BF16),
        grid=(T // tg,),
        in_specs=[sspec, sspec, sspec],
        out_specs=pl.BlockSpec((tg, N_EXPERTS), lambda i: (i, 0)),
        compiler_params=pltpu.CompilerParams(dimension_semantics=("arbitrary",), vmem_limit_bytes=VMEM_LIMIT),
        name="peer_gate",
    )(i1, i2, w)


def _peer_expert_kernel(h_ref, ut_ref, v_ref, g_ref, x_ref, nf_ref, y_ref, acc_ref):
    e = pl.program_id(1)

    @pl.when(e == 0)
    def _():
        acc_ref[...] = jnp.zeros(acc_ref.shape, F32)

    x = jnp.dot(h_ref[...], ut_ref[...], preferred_element_type=F32)
    act = 0.5 * x * (1.0 + lax.erf(x * (2.0 ** -0.5)))
    ga = (g_ref[...].astype(F32) * act).astype(BF16)
    acc_ref[...] += jnp.dot(ga, v_ref[...], preferred_element_type=F32)

    @pl.when(e == pl.num_programs(1) - 1)
    def _():
        xo = x_ref[...] + acc_ref[...]
        ms = jnp.mean(xo * xo, axis=-1, keepdims=True)
        y_ref[...] = xo * lax.rsqrt(ms + EPS) * nf_ref[...]


def _peer_expert(h, ut, v, g, x1, nf, tt, et):
    T = x1.shape[0]
    return pl.pallas_call(
        _peer_expert_kernel,
        out_shape=jax.ShapeDtypeStruct((T, D_MODEL), F32),
        grid=(T // tt, N_EXPERTS // et),
        in_specs=[pl.BlockSpec((tt, D_MODEL), lambda i, e: (i, 0)),
                  pl.BlockSpec((None, D_MODEL, et), lambda i, e: (0, 0, e)),
                  pl.BlockSpec((None, et, D_MODEL), lambda i, e: (0, e, 0)),
                  pl.BlockSpec((tt, et), lambda i, e: (i, e)),
                  pl.BlockSpec((tt, D_MODEL), lambda i, e: (i, 0), pipeline_mode=pl.Buffered(1)),
                  pl.BlockSpec(nf.shape, lambda i, e: (0, 0))],
        out_specs=pl.BlockSpec((tt, D_MODEL), lambda i, e: (i, 0)),
        scratch_shapes=[pltpu.VMEM((tt, D_MODEL), F32)],
        compiler_params=pltpu.CompilerParams(dimension_semantics=("arbitrary", "arbitrary"),
                                             vmem_limit_bytes=VMEM_LIMIT),
        name="peer_expert",
    )(h, ut, v, g, x1, nf)


def _pick_tile(T, prefs):
    for t in prefs:
        if T % t == 0:
            return t
    raise ValueError(f"no tile for {T}")


def _layer(x, conv_prev, ssm_prev, k_cache, v_cache, lam, lam_init, wts):
    (n1, w_main, w_ba, w_bat, conv_w, arow, drow, acol, dcol, gnw, sw, w_out, n2, wq, kblk, ut, v, nf) = wts
    B, T, _ = x.shape
    n_tok = B * T
    x2d = x.reshape(n_tok, D_MODEL)
    tm = _pick_tile(n_tok, (256, 128))
    conv_in, z, qb, kb, vb, kbh, vbh, ba, bat = _inproj(x2d, n1, w_main, w_ba, w_bat, tm)

    C = min(CHUNK, T)
    prev8 = jnp.pad(conv_prev.astype(F32), ((0, 0), (SUBLANES - CONV_W + 1, 0), (0, 0)))
    bat4 = bat.reshape(SUBLANES, B, T // C, C).transpose(1, 2, 0, 3)
    o_a, ssm_new = _gdn(conv_in.reshape(B, T, CONV_CH), z.reshape(B, T, W_A), ba.reshape(B, T, LANES), bat4,
                        prev8, ssm_prev.astype(F32), conv_w, arow, drow, acol, dcol, gnw, C,
                        _pick_tile(T // C, (4, 2, 1)))
    conv_new = conv_in.reshape(B, T, CONV_CH)[:, T - (CONV_W - 1):, :]

    if k_cache is None:
        tq = _pick_tile(T, (1024, 512, 256, 128))
        o_b = _attn_prompt(lam, qb, kbh, vbh, sw, tq, lam_init)
    else:
        o_b = _attn_sample(lam, qb.reshape(B, T, W_B), kbh.reshape(B, T, W_B), vbh.reshape(B, T, W_B),
                           k_cache, v_cache, sw, lam_init)
        o_b = o_b.reshape(n_tok, W_B)

    tt = _pick_tile(n_tok, (512, 256, 128))
    x1, h2, i1, i2, gate = _peer_score(x2d, o_a.reshape(n_tok, W_A), o_b, w_out, n2, wq, kblk, tt)
    g = _peer_gate(i1, i2, gate, _pick_tile(n_tok, (128,)))
    y = _peer_expert(h2, ut, v, g, x1, nf, _pick_tile(n_tok, (1024, 512, 256, 128)), PEER_NA * N_KEYS)
    return (y.reshape(B, T, D_MODEL), kb.reshape(B, T, H_B, 2 * DH_B), vb.reshape(B, T, H_B, DV_B),
            conv_new, ssm_new)


def _prep_weights(norm1_w, w_in, conv_w, a_log, dt_bias, gdn_norm_w, subln_w, w_out,
                  norm2_w, peer_wq, peer_k1, peer_k2, peer_u, peer_v, norm_f_w):
    o_beta = CONV_CH + W_A
    w_main = jnp.concatenate([w_in[:, :o_beta], w_in[:, o_beta + 2 * H_A:]], axis=1).astype(BF16)
    w_gate = w_in[:, o_beta:o_beta + 2 * H_A]
    w_ba = jnp.pad(w_gate, ((0, 0), (0, LANES - 2 * H_A))).astype(BF16)
    w_bat = w_gate.T.astype(BF16)
    zeros4 = jnp.zeros((H_A,), F32)
    a8 = jnp.concatenate([zeros4, a_log.astype(F32)])
    d8 = jnp.concatenate([zeros4, dt_bias.astype(F32)])
    arow = jnp.pad(a8, (0, LANES - 2 * H_A)).reshape(1, LANES)
    drow = jnp.pad(d8, (0, LANES - 2 * H_A)).reshape(1, LANES)
    acol = a8.reshape(2 * H_A, 1)
    dcol = d8.reshape(2 * H_A, 1)
    halves = jnp.stack([peer_k1, peer_k2]).astype(F32)
    eye = jnp.eye(2, dtype=F32)
    kblk = jnp.einsum("gj,gnd->gnjd", eye, halves).reshape(2 * N_KEYS, 2 * PEER_DK_HALF).astype(BF16)
    return (norm1_w.reshape(1, D_MODEL), w_main, w_ba, w_bat, conv_w, arow, drow, acol, dcol,
            gdn_norm_w.reshape(1, DK_A), subln_w.reshape(1, DV_B), w_out.astype(BF16),
            norm2_w.reshape(1, D_MODEL), peer_wq.astype(BF16), kblk,
            jnp.swapaxes(peer_u, 1, 2).astype(BF16), peer_v.astype(BF16), norm_f_w.reshape(1, D_MODEL))


def kernel(x_prompt, x_sample, cache_k, cache_v, state_conv, state_gdn, norm1_w, w_in, conv_w, a_log, dt_bias, gdn_norm_w, lam_q1, lam_k1, lam_q2, lam_k2, subln_w, w_out, norm2_w, peer_wq, peer_k1, peer_k2, peer_u, peer_v, norm_f_w):
    depth = w_in.shape[0]
    assert depth == 1, "the final norm is fused into the (single) layer's last kernel"
    l = 0
    lam_init = 0.8 - 0.6 * math.exp(-0.3 * l)
    lam = (jnp.exp(jnp.sum(lam_q1[l].astype(F32) * lam_k1[l].astype(F32)))
           - jnp.exp(jnp.sum(lam_q2[l].astype(F32) * lam_k2[l].astype(F32))) + lam_init).reshape(1)
    layer = lambda a: a.reshape(a.shape[1:])
    wts = _prep_weights(layer(norm1_w), layer(w_in), layer(conv_w), layer(a_log), layer(dt_bias),
                        layer(gdn_norm_w), layer(subln_w), layer(w_out), layer(norm2_w), layer(peer_wq),
                        layer(peer_k1), layer(peer_k2), peer_u, peer_v, norm_f_w)
    bp = x_prompt.shape[0]
    conv0 = jnp.zeros((bp, CONV_W - 1, CONV_CH), F32)
    ssm0 = jnp.zeros((bp, H_A, DK_A, DK_A), F32)
    yp, k1, v1, c1, s1 = _layer(x_prompt, conv0, ssm0, None, None, lam, lam_init, wts)
    ys, k2, v2, c2, s2 = _layer(x_sample, layer(state_conv), layer(state_gdn), layer(cache_k), layer(cache_v),
                                lam, lam_init, wts)
    st = lambda a: a[None]
    return (yp, ys, st(k1), st(v1), st(c1), st(s1), st(k2), st(v2), st(c2), st(s2))
```
